```python
import math
import jax, jax.numpy as jnp
from jax import lax
import numpy as np

D_MODEL = 1024
BATCH = 16
SEQ = 4096
DEPTH = 1

HEAD_DIM = 64
MIX_WIDTH = D_MODEL
A_HEADS = (MIX_WIDTH // 2) // HEAD_DIM
A_KV_HEADS = 2
A_GROUP = A_HEADS // A_KV_HEADS
B_HEADS = (MIX_WIDTH // 2) // HEAD_DIM
WINDOW = 128
BLOCK = 128
D_FF = 4 * D_MODEL
EPS = 1e-6

A_Q_W = A_HEADS * HEAD_DIM
A_KV_W = A_KV_HEADS * HEAD_DIM
B_W = B_HEADS * HEAD_DIM
IN_SPLITS = tuple(np.cumsum([A_Q_W, A_KV_W, A_KV_W, B_W, B_W, B_W]).tolist())
IN_WIDTH = A_Q_W + 2 * A_KV_W + 3 * B_W + B_HEADS

kernel_name = "hybrid_swa_sinks_fox_sqrelu"


def rmsnorm(x, g):
    x32 = x.astype(jnp.float32)
    y = x32 * lax.rsqrt(jnp.mean(x32 * x32, axis=-1, keepdims=True) + EPS)
    return (y * g.astype(jnp.float32)).astype(x.dtype)


def alibi_slopes(n):
    return jnp.exp2(-(8.0 / n) * (jnp.arange(n, dtype=jnp.float32) + 1.0))


def swa_sinks_attention(q, k, v, sinks):
    b, s, _, d = q.shape
    nb = s // BLOCK
    scale = 1.0 / math.sqrt(d)
    qb = q.reshape(b, nb, BLOCK, A_KV_HEADS, A_GROUP, d)
    pad = ((0, 0), (BLOCK, 0), (0, 0), (0, 0))
    kp = jnp.pad(k, pad).reshape(b, nb + 1, BLOCK, A_KV_HEADS, d)
    vp = jnp.pad(v, pad).reshape(b, nb + 1, BLOCK, A_KV_HEADS, d)
    kb = jnp.concatenate([kp[:, :-1], kp[:, 1:]], axis=2)
    vb = jnp.concatenate([vp[:, :-1], vp[:, 1:]], axis=2)
    scores = jnp.einsum('bnqkgd,bnskd->bnkgqs', qb, kb).astype(jnp.float32) * scale
    qpos = BLOCK + jnp.arange(BLOCK)
    kpos = jnp.arange(2 * BLOCK)
    dist = qpos[:, None] - kpos[None, :]
    band = (dist >= 0) & (dist < WINDOW)
    first_pad = (jnp.arange(nb) == 0)[:, None, None] & (kpos < BLOCK)[None, None, :]
    valid = band[None] & ~first_pad
    slopes = alibi_slopes(A_HEADS).reshape(A_KV_HEADS, A_GROUP)
    alibi = -slopes[:, :, None, None] * dist.astype(jnp.float32)[None, None]
    scores = scores + alibi[None, None]
    scores = jnp.where(valid[None, :, None, None], scores, -jnp.inf)
    sink = sinks.astype(jnp.float32).reshape(1, 1, A_KV_HEADS, A_GROUP, 1, 1)
    m = jnp.maximum(jnp.max(scores, axis=-1, keepdims=True), sink)
    p = jnp.exp(scores - m)
    denom = jnp.sum(p, axis=-1, keepdims=True) + jnp.exp(sink - m)
    p = (p / denom).astype(v.dtype)
    out = jnp.einsum('bnkgqs,bnskd->bnqkgd', p, vb)
    return out.reshape(b, s, A_HEADS, d)


def forgetting_attention(q, k, v, log_f):
    b, s, h, d = q.shape
    nb = s // BLOCK
    scale = 1.0 / math.sqrt(d)
    c = jnp.cumsum(log_f, axis=1)
    c_keys = jnp.transpose(c, (0, 2, 1))
    qb = jnp.moveaxis(q.reshape(b, nb, BLOCK, h, d), 1, 0)
    cb = jnp.moveaxis(c.reshape(b, nb, BLOCK, h), 1, 0)
    kpos = jnp.arange(s)

    def block_step(args):
        qi, ci, i = args
        sc = jnp.einsum('bqhd,bshd->bhqs', qi, k).astype(jnp.float32) * scale
        bias = jnp.transpose(ci, (0, 2, 1))[..., None] - c_keys[:, :, None, :]
        qpos = i * BLOCK + jnp.arange(BLOCK)
        causal = kpos[None, :] <= qpos[:, None]
        sc = jnp.where(causal[None, None], sc + bias, -jnp.inf)
        p = jax.nn.softmax(sc, axis=-1).astype(v.dtype)
        return jnp.einsum('bhqs,bshd->bqhd', p, v)

    out = lax.map(block_step, (qb, cb, jnp.arange(nb)))
    return jnp.moveaxis(out, 0, 1).reshape(b, s, h, d)


def setup_inputs(seed: int = 0) -> dict:
    key = jax.random.key(seed)
    ks = jax.random.split(key, 14)
    f32 = jnp.float32
    x = jax.random.normal(ks[0], (BATCH, SEQ, D_MODEL), f32)
    attn_norm_g = 1.0 + 0.02 * jax.random.normal(ks[1], (D_MODEL,), f32)
    w_in = jax.random.normal(ks[2], (D_MODEL, IN_WIDTH), f32) * D_MODEL ** -0.5
    b_forget = 2.0 + 0.5 * jax.random.normal(ks[3], (B_HEADS,), f32)
    q_norm_a = 1.0 + 0.02 * jax.random.normal(ks[4], (HEAD_DIM,), f32)
    k_norm_a = 1.0 + 0.02 * jax.random.normal(ks[5], (HEAD_DIM,), f32)
    sink_logits = 0.5 * jax.random.normal(ks[6], (A_HEADS,), f32)
    q_norm_b = 1.0 + 0.02 * jax.random.normal(ks[7], (HEAD_DIM,), f32)
    k_norm_b = 1.0 + 0.02 * jax.random.normal(ks[8], (HEAD_DIM,), f32)
    w_out = jax.random.normal(ks[9], (MIX_WIDTH, D_MODEL), f32) * MIX_WIDTH ** -0.5
    mlp_norm_g = 1.0 + 0.02 * jax.random.normal(ks[10], (D_MODEL,), f32)
    w_up = jax.random.normal(ks[11], (D_MODEL, D_FF), f32) * D_MODEL ** -0.5
    w_down = jax.random.normal(ks[12], (D_FF, D_MODEL), f32) * D_FF ** -0.5
    return {"x": x, "attn_norm_g": attn_norm_g, "w_in": w_in, "b_forget": b_forget,
            "q_norm_a": q_norm_a, "k_norm_a": k_norm_a, "sink_logits": sink_logits,
            "q_norm_b": q_norm_b, "k_norm_b": k_norm_b, "w_out": w_out,
            "mlp_norm_g": mlp_norm_g, "w_up": w_up, "w_down": w_down}


def reference(x, attn_norm_g, w_in, b_forget, q_norm_a, k_norm_a, sink_logits,
              q_norm_b, k_norm_b, w_out, mlp_norm_g, w_up, w_down):
    b, s, _ = x.shape
    for _layer in range(DEPTH):
        xn = rmsnorm(x, attn_norm_g)
        proj = jnp.einsum('bsd,de->bse', xn, w_in)
        qa, ka, va, qb, kb, vb, f_logit = jnp.split(proj, IN_SPLITS, axis=-1)
        qa = rmsnorm(qa.reshape(b, s, A_HEADS, HEAD_DIM), q_norm_a)
        ka = rmsnorm(ka.reshape(b, s, A_KV_HEADS, HEAD_DIM), k_norm_a)
        va = va.reshape(b, s, A_KV_HEADS, HEAD_DIM)
        out_a = swa_sinks_attention(qa, ka, va, sink_logits)
        qb = rmsnorm(qb.reshape(b, s, B_HEADS, HEAD_DIM), q_norm_b)
        kb = rmsnorm(kb.reshape(b, s, B_HEADS, HEAD_DIM), k_norm_b)
        vb = vb.reshape(b, s, B_HEADS, HEAD_DIM)
        log_f = jax.nn.log_sigmoid(f_logit.astype(jnp.float32) + b_forget.astype(jnp.float32))
        out_b = forgetting_attention(qb, kb, vb, log_f)
        mixed = jnp.concatenate([out_a.reshape(b, s, A_Q_W), out_b.reshape(b, s, B_W)], axis=-1)
        x = x + jnp.einsum('bse,ed->bsd', mixed, w_out)
        hn = rmsnorm(x, mlp_norm_g)
        hid = jnp.square(jax.nn.relu(jnp.einsum('bsd,df->bsf', hn, w_up)))
        x = x + jnp.einsum('bsf,fd->bsd', hid, w_down)
    return x
```

```python
import functools
import math

import numpy as np
import jax
import jax.numpy as jnp
from jax import lax
from jax.experimental import pallas as pl
from jax.experimental.pallas import tpu as pltpu

F32 = jnp.float32
BF16 = jnp.bfloat16

D_MODEL = 1024
HEAD_DIM = 64
A_HEADS = 8
A_KV_HEADS = 2
A_GROUP = A_HEADS // A_KV_HEADS
B_HEADS = 8
WINDOW = 128
D_FF = 4 * D_MODEL
EPS = 1e-6
LOG2E = 1.4426950408889634
NEG_BIG = -1e30

LANES = 128
HALF = LANES // 2

QA_BLK = 0
KA_BLK = 4
VA_BLK = 5
QB_BLK = 6
KB_BLK = 10
VB_BLK = 14
QKV_W = 18 * LANES
PROJ_W = QKV_W + LANES

PIECE_STRIDE = 8
ONE_LANE = 3 * PIECE_STRIDE

TOK_TILE = 512
Q_TILE = 512
K_TILE = 512
PRO_TILE = 512
SUB = 128


def _lane_iota(shape):
    return lax.broadcasted_iota(jnp.int32, shape, len(shape) - 1)


def _split3(v):
    hi = v.astype(BF16).astype(F32)
    r = v - hi
    mid = r.astype(BF16).astype(F32)
    lo = (r - mid).astype(BF16).astype(F32)
    return hi, mid, lo


def _head_rms_scale(t, e2):
    sq = t * t
    hi = sq.astype(BF16)
    lo = (sq - hi.astype(F32)).astype(BF16)
    ss = (jnp.dot(hi, e2, preferred_element_type=F32) + jnp.dot(lo, e2, preferred_element_type=F32))
    return lax.rsqrt(ss * (1.0 / HEAD_DIM) + EPS)


def _in_proj_kernel(x_ref, g_ref, w_ref, bf_ref, tri_ref, qkv_ref, cp_ref, carry_ref):
    @pl.when(pl.program_id(1) == 0)
    def _():
        carry_ref[...] = jnp.zeros_like(carry_ref)

    x = x_ref[0]
    ms = jnp.sum(x * x, axis=-1, keepdims=True) * (1.0 / D_MODEL)
    xn = (x * lax.rsqrt(ms + EPS) * g_ref[...]).astype(BF16)
    proj = jnp.dot(xn, w_ref[...], preferred_element_type=F32)
    qkv_ref[0] = proj[:, :QKV_W].astype(BF16)

    z = proj[:, QKV_W:] + bf_ref[...]
    lane = _lane_iota(z.shape)
    logf = jnp.minimum(z, 0.0) - jnp.log(1.0 + jnp.exp(-jnp.abs(z)))
    logf = jnp.where(lane < B_HEADS, logf * LOG2E, 0.0)

    hi, mid, lo = _split3(logf)
    packed = hi + pltpu.roll(mid, PIECE_STRIDE, 1) + pltpu.roll(lo, 2 * PIECE_STRIDE, 1)
    cs = jnp.dot(tri_ref[...], packed.astype(BF16), preferred_element_type=F32)
    cin = cs + pltpu.roll(cs, LANES - PIECE_STRIDE, 1) + pltpu.roll(cs, LANES - 2 * PIECE_STRIDE, 1)
    c = jnp.where(lane < B_HEADS, cin + carry_ref[...], 0.0)
    carry_ref[...] = c[c.shape[0] - 1:, :]

    chi, cmid, clo = _split3(c)
    pieces = (chi + pltpu.roll(cmid, PIECE_STRIDE, 1) + pltpu.roll(clo, 2 * PIECE_STRIDE, 1)
              + jnp.where(lane == ONE_LANE, 1.0, 0.0))
    cp_ref[0] = pieces.astype(BF16)


def _in_proj(x, g, w_pad, bf_pad, tri):
    b, s, d = x.shape
    tm = TOK_TILE
    const = lambda bi, si: (0, 0)
    return pl.pallas_call(
        _in_proj_kernel,
        grid=(b, s // tm),
        in_specs=[
            pl.BlockSpec((1, tm, d), lambda bi, si: (bi, si, 0)),
            pl.BlockSpec((1, d), const),
            pl.BlockSpec((d, PROJ_W), const),
            pl.BlockSpec((1, LANES), const),
            pl.BlockSpec((tm, tm), const),
        ],
        out_specs=[
            pl.BlockSpec((1, tm, QKV_W), lambda bi, si: (bi, si, 0)),
            pl.BlockSpec((1, tm, LANES), lambda bi, si: (bi, si, 0)),
        ],
        out_shape=[
            jax.ShapeDtypeStruct((b, s, QKV_W), BF16),
            jax.ShapeDtypeStruct((b, s, LANES), BF16),
        ],
        scratch_shapes=[pltpu.VMEM((1, LANES), F32)],
        compiler_params=pltpu.CompilerParams(
            dimension_semantics=("arbitrary", "arbitrary"),
            vmem_limit_bytes=48 * 1024 * 1024),
        name="in_proj",
    )(x, g, w_pad, bf_pad, tri)


def _swa_kernel(q_ref, kc_ref, kp_ref, vc_ref, vp_ref, gq_ref, gk_ref, e2_ref, bias_ref, sink_ref, o_ref):
    g = pl.program_id(1)
    qi = pl.program_id(2)
    tq = q_ref.shape[1]
    e2 = e2_ref[...]
    lane = _lane_iota((1, LANES))
    low = lane < HALF

    def kv_low(t):
        return jnp.where(g == 0, t, pltpu.roll(t, HALF, 1))

    k2 = jnp.concatenate([kp_ref[0], kc_ref[0]], axis=0).astype(F32)
    kn = k2 * _head_rms_scale(k2, e2) * gk_ref[...]
    k_lo = jnp.where(low, kv_low(kn), 0.0).astype(BF16)
    v2 = jnp.concatenate([vp_ref[0], vc_ref[0]], axis=0).astype(F32)
    v_lo = jnp.where(low, kv_low(v2), jnp.where(lane == HALF, 1.0, 0.0)).astype(BF16)

    q_heads = []
    for c in range(A_GROUP // 2):
        qc = q_ref[0, :, c * LANES:(c + 1) * LANES].astype(F32)
        qn = qc * _head_rms_scale(qc, e2) * gq_ref[...]
        q_heads.append(jnp.where(low, qn, 0.0).astype(BF16))
        q_heads.append(jnp.where(low, pltpu.roll(qn, HALF, 1), 0.0).astype(BF16))

    sink = sink_ref[0]
    outs = [[] for _ in range(A_GROUP)]
    for r in range(tq // SUB):
        q_st = jnp.concatenate([qh[r * SUB:(r + 1) * SUB] for qh in q_heads], axis=0)
        kw = k_lo[r * SUB:r * SUB + 2 * SUB]
        vw = v_lo[r * SUB:r * SUB + 2 * SUB]
        s = lax.dot_general(q_st, kw, (((1,), (1,)), ((), ())), preferred_element_type=F32)
        if r == 0:
            first = (qi == 0)
            s = s + jnp.where(first, bias_ref[0, 1], bias_ref[0, 0])
        else:
            s = s + bias_ref[0, 0]
        m = jnp.maximum(jnp.max(s, axis=-1, keepdims=True), sink[:, :1])
        p = jnp.exp(s - m)
        pv = jnp.dot(p.astype(BF16), vw, preferred_element_type=F32)
        denom = pv[:, HALF:HALF + 1] + jnp.exp(sink[:, :1] - m)
        o = pv * (1.0 / denom)
        for j in range(A_GROUP):
            outs[j].append(o[j * SUB:(j + 1) * SUB])
    heads = [jnp.concatenate(o, axis=0) for o in outs]
    for c in range(A_GROUP // 2):
        o_ref[0, :, c * LANES:(c + 1) * LANES] = jnp.where(
            low, heads[2 * c], pltpu.roll(heads[2 * c + 1], HALF, 1)).astype(o_ref.dtype)


def _swa(qkv, gq2, gk2, e2, bias, sink_rows):
    b, s, _ = qkv.shape
    tq = Q_TILE
    nsub = tq // SUB
    grp_w = A_GROUP * HEAD_DIM
    return pl.pallas_call(
        _swa_kernel,
        grid=(b, A_KV_HEADS, s // tq),
        in_specs=[
            pl.BlockSpec((1, tq, grp_w), lambda bi, g, qi: (bi, qi, g)),
            pl.BlockSpec((1, tq, LANES), lambda bi, g, qi: (bi, qi, KA_BLK)),
            pl.BlockSpec((1, SUB, LANES), lambda bi, g, qi: (bi, jnp.maximum(qi * nsub - 1, 0), KA_BLK)),
            pl.BlockSpec((1, tq, LANES), lambda bi, g, qi: (bi, qi, VA_BLK)),
            pl.BlockSpec((1, SUB, LANES), lambda bi, g, qi: (bi, jnp.maximum(qi * nsub - 1, 0), VA_BLK)),
            pl.BlockSpec((1, LANES), lambda bi, g, qi: (0, 0)),
            pl.BlockSpec((1, LANES), lambda bi, g, qi: (0, 0)),
            pl.BlockSpec((LANES, LANES), lambda bi, g, qi: (0, 0)),
            pl.BlockSpec((1, 2, A_GROUP * SUB, 2 * SUB), lambda bi, g, qi: (g, 0, 0, 0)),
            pl.BlockSpec((1, A_GROUP * SUB, LANES), lambda bi, g, qi: (g, 0, 0)),
        ],
        out_specs=pl.BlockSpec((1, tq, grp_w), lambda bi, g, qi: (bi, qi, g)),
        out_shape=jax.ShapeDtypeStruct((b, s, A_HEADS * HEAD_DIM), BF16),
        compiler_params=pltpu.CompilerParams(
            dimension_semantics=("arbitrary", "arbitrary", "arbitrary"),
            vmem_limit_bytes=48 * 1024 * 1024),
        name="swa",
    )(qkv, qkv, qkv, qkv, qkv, gq2, gk2, e2, bias, sink_rows)


def _fox_kernel(q_ref, k_ref, v_ref, cpq_ref, cpk_ref, selq_ref, selk_ref, gq_ref, gk_ref, e2_ref,
                o_ref, kaug_ref, vaug_ref, m_ref, acc_ref):
    qi = pl.program_id(2)
    tq = q_ref.shape[1]
    s_len = k_ref.shape[1]
    tk = K_TILE
    e2 = e2_ref[...]
    lane = _lane_iota((1, LANES))
    low = lane < HALF
    own = (low, jnp.logical_not(low))
    one_lane = (HALF, 0)

    @pl.when(qi == 0)
    def _prepare_keys():
        def chunk(i, _):
            r0 = pl.multiple_of(i * PRO_TILE, PRO_TILE)
            rows = pl.ds(r0, PRO_TILE)
            kc = k_ref[0, rows, :].astype(F32)
            kn = kc * _head_rms_scale(kc, e2) * gk_ref[...]
            vc = v_ref[0, rows, :].astype(F32)
            cp = cpk_ref[0, rows, :]
            for hh in range(2):
                extra = jnp.dot(cp, selk_ref[0, hh], preferred_element_type=F32)
                kaug_ref[hh, rows, :] = jnp.where(own[hh], kn, extra).astype(BF16)
                vaug_ref[hh, rows, :] = jnp.where(
                    own[hh], vc, jnp.where(lane == one_lane[hh], 1.0, 0.0)).astype(BF16)
            return 0
        lax.fori_loop(0, s_len // PRO_TILE, chunk, 0)

    qc = q_ref[0].astype(F32)
    qn = qc * _head_rms_scale(qc, e2) * gq_ref[...]
    cpq = cpq_ref[0]

    row = lax.broadcasted_iota(jnp.int32, (tq, tk), 0)
    col = lax.broadcasted_iota(jnp.int32, (tq, tk), 1)
    causal = col <= row

    results = []
    for hh in range(2):
        extra = jnp.dot(cpq, selq_ref[0, hh], preferred_element_type=F32)
        qa = jnp.where(own[hh], qn, extra).astype(BF16)
        m_ref[...] = jnp.full_like(m_ref, NEG_BIG)
        acc_ref[...] = jnp.zeros_like(acc_ref)

        def step(kb, masked):
            rows = pl.ds(pl.multiple_of(kb * tk, tk), tk)
            s = lax.dot_general(qa, kaug_ref[hh, rows, :], (((1,), (1,)), ((), ())),
                                preferred_element_type=F32)
            if masked:
                s = jnp.where(causal, s, NEG_BIG)
            m_old = m_ref[...]
            m_new = jnp.maximum(m_old, jnp.max(s, axis=-1, keepdims=True))
            p = jnp.exp2(s - m_new[:, :1])
            acc_ref[...] = (jnp.exp2(m_old - m_new) * acc_ref[...]
                            + jnp.dot(p.astype(BF16), vaug_ref[hh, rows, :], preferred_element_type=F32))
            m_ref[...] = m_new

        def body(kb, _):
            step(kb, False)
            return 0
        lax.fori_loop(0, qi, body, 0)
        step(qi, True)

        acc = acc_ref[...]
        results.append(acc * (1.0 / acc[:, one_lane[hh]:one_lane[hh] + 1]))
    o_ref[0] = jnp.where(low, results[0], results[1]).astype(o_ref.dtype)


def _fox(qkv, cp, selq, selk, gq2, gk2, e2):
    b, s, _ = qkv.shape
    tq = Q_TILE
    npair = B_HEADS // 2
    return pl.pallas_call(
        _fox_kernel,
        grid=(b, npair, s // tq),
        in_specs=[
            pl.BlockSpec((1, tq, LANES), lambda bi, p, qi: (bi, qi, QB_BLK + p)),
            pl.BlockSpec((1, s, LANES), lambda bi, p, qi: (bi, 0, KB_BLK + p)),
            pl.BlockSpec((1, s, LANES), lambda bi, p, qi: (bi, 0, VB_BLK + p)),
            pl.BlockSpec((1, tq, LANES), lambda bi, p, qi: (bi, qi, 0)),
            pl.BlockSpec((1, s, LANES), lambda bi, p, qi: (bi, 0, 0)),
            pl.BlockSpec((1, 2, LANES, LANES), lambda bi, p, qi: (p, 0, 0, 0)),
            pl.BlockSpec((1, 2, LANES, LANES), lambda bi, p, qi: (p, 0, 0, 0)),
            pl.BlockSpec((1, LANES), lambda bi, p, qi: (0, 0)),
            pl.BlockSpec((1, LANES), lambda bi, p, qi: (0, 0)),
            pl.BlockSpec((LANES, LANES), lambda bi, p, qi: (0, 0)),
        ],
        out_specs=pl.BlockSpec((1, tq, LANES), lambda bi, p, qi: (bi, qi, p)),
        out_shape=jax.ShapeDtypeStruct((b, s, B_HEADS * HEAD_DIM), BF16),
        scratch_shapes=[
            pltpu.VMEM((2, s, LANES), BF16),
            pltpu.VMEM((2, s, LANES), BF16),
            pltpu.VMEM((tq, LANES), F32),
            pltpu.VMEM((tq, LANES), F32),
        ],
        compiler_params=pltpu.CompilerParams(
            dimension_semantics=("arbitrary", "arbitrary", "arbitrary"),
            vmem_limit_bytes=48 * 1024 * 1024),
        name="fox",
    )(qkv, qkv, qkv, cp, cp, selq, selk, gq2, gk2, e2)


def _out_mlp_kernel(x_ref, oa_ref, ob_ref, woa_ref, wob_ref, g_ref, wup_ref, wdn_ref, y_ref):
    h = (x_ref[...]
         + jnp.dot(oa_ref[...], woa_ref[...], preferred_element_type=F32)
         + jnp.dot(ob_ref[...], wob_ref[...], preferred_element_type=F32))
    ms = jnp.sum(h * h, axis=-1, keepdims=True) * (1.0 / D_MODEL)
    hn = (h * lax.rsqrt(ms + EPS) * g_ref[...]).astype(BF16)
    u = jnp.maximum(jnp.dot(hn, wup_ref[...], preferred_element_type=F32), 0.0)
    y_ref[...] = h + jnp.dot((u * u).astype(BF16), wdn_ref[...], preferred_element_type=F32)


def _out_mlp(x2, oa2, ob2, woa, wob, g2, wup, wdn):
    n, d = x2.shape
    tm = TOK_TILE
    const = lambda i: (0, 0)
    resident = dict(pipeline_mode=pl.Buffered(1))
    return pl.pallas_call(
        _out_mlp_kernel,
        grid=(n // tm,),
        in_specs=[
            pl.BlockSpec((tm, d), lambda i: (i, 0)),
            pl.BlockSpec((tm, oa2.shape[1]), lambda i: (i, 0)),
            pl.BlockSpec((tm, ob2.shape[1]), lambda i: (i, 0)),
            pl.BlockSpec(woa.shape, const, **resident),
            pl.BlockSpec(wob.shape, const, **resident),
            pl.BlockSpec((1, d), const),
            pl.BlockSpec(wup.shape, const, **resident),
            pl.BlockSpec(wdn.shape, const, **resident),
        ],
        out_specs=pl.BlockSpec((tm, d), lambda i: (i, 0)),
        out_shape=jax.ShapeDtypeStruct((n, d), F32),
        compiler_params=pltpu.CompilerParams(
            dimension_semantics=("arbitrary",),
            vmem_limit_bytes=52 * 1024 * 1024),
        name="out_mlp",
    )(x2, oa2, ob2, woa, wob, g2, wup, wdn)


def _block_diag_ones():
    i = np.arange(LANES)
    return jnp.asarray((i[:, None] // HALF == i[None, :] // HALF).astype(np.float32), dtype=BF16)


def _fox_select_tables():
    selq = np.zeros((B_HEADS // 2, 2, LANES, LANES), np.float32)
    selk = np.zeros_like(selq)
    for p in range(B_HEADS // 2):
        for hh in range(2):
            h = 2 * p + hh
            base = HALF if hh == 0 else 0
            for piece in range(3):
                selq[p, hh, piece * PIECE_STRIDE + h, base + piece] = 1.0
                selq[p, hh, ONE_LANE, base + 3 + piece] = 1.0
                selk[p, hh, ONE_LANE, base + piece] = 1.0
                selk[p, hh, piece * PIECE_STRIDE + h, base + 3 + piece] = -1.0
    return jnp.asarray(selq, dtype=BF16), jnp.asarray(selk, dtype=BF16)


def _swa_bias_tables():
    slopes = np.exp2(-(8.0 / A_HEADS) * (np.arange(A_HEADS, dtype=np.float32) + 1.0)).astype(np.float32)
    qpos = SUB + np.arange(SUB)
    kpos = np.arange(2 * SUB)
    dist = qpos[:, None] - kpos[None, :]
    band = (dist >= 0) & (dist < WINDOW)
    out = np.zeros((A_KV_HEADS, 2, A_GROUP * SUB, 2 * SUB), np.float32)
    for g in range(A_KV_HEADS):
        for j in range(A_GROUP):
            alibi = (-slopes[g * A_GROUP + j] * dist.astype(np.float32)).astype(np.float32)
            rows = slice(j * SUB, (j + 1) * SUB)
            out[g, 0, rows] = np.where(band, alibi, NEG_BIG)
            out[g, 1, rows] = np.where(band & (kpos >= SUB)[None, :], alibi, NEG_BIG)
    return jnp.asarray(out)


def kernel(x, attn_norm_g, w_in, b_forget, q_norm_a, k_norm_a, sink_logits, q_norm_b, k_norm_b, w_out,
           mlp_norm_g, w_up, w_down):
    b, s, d = x.shape
    assert d == D_MODEL and s % Q_TILE == 0 and s % TOK_TILE == 0
    scale = 1.0 / math.sqrt(HEAD_DIM)

    w_pad = jnp.pad(w_in, ((0, 0), (0, PROJ_W - w_in.shape[1]))).astype(BF16)
    bf_pad = jnp.pad(b_forget.astype(F32), (0, LANES - B_HEADS)).reshape(1, LANES)
    tri = jnp.asarray(np.tril(np.ones((TOK_TILE, TOK_TILE), np.float32)), dtype=BF16)
    e2 = _block_diag_ones()
    selq, selk = _fox_select_tables()
    bias = _swa_bias_tables()
    sink_rows = jnp.broadcast_to(
        jnp.repeat(sink_logits.astype(F32).reshape(A_KV_HEADS, A_GROUP), SUB, axis=1)[:, :, None],
        (A_KV_HEADS, A_GROUP * SUB, LANES))
    gqa2 = (jnp.tile(q_norm_a.astype(F32), 2) * scale).reshape(1, LANES)
    gka2 = jnp.tile(k_norm_a.astype(F32), 2).reshape(1, LANES)
    gqb2 = (jnp.tile(q_norm_b.astype(F32), 2) * (scale * LOG2E)).reshape(1, LANES)
    gkb2 = jnp.tile(k_norm_b.astype(F32), 2).reshape(1, LANES)

    qkv, cp = _in_proj(x, attn_norm_g.astype(F32).reshape(1, d), w_pad, bf_pad, tri)
    out_a = _swa(qkv, gqa2, gka2, e2, bias, sink_rows)
    out_b = _fox(qkv, cp, selq, selk, gqb2, gkb2, e2)

    wo = w_out.astype(BF16)
    y = _out_mlp(x.reshape(b * s, d), out_a.reshape(b * s, -1), out_b.reshape(b * s, -1),
                 wo[:A_HEADS * HEAD_DIM], wo[A_HEADS * HEAD_DIM:],
                 mlp_norm_g.astype(F32).reshape(1, d), w_up.astype(BF16), w_down.astype(BF16))
    return y.reshape(b, s, d)
```

```python
import functools
import math

import numpy as np
import jax
import jax.numpy as jnp
from jax import lax
from jax.experimental import pallas as pl
from jax.experimental.pallas import tpu as pltpu

F32 = jnp.float32
BF16 = jnp.bfloat16

D_MODEL = 1024
HEAD_DIM = 64
A_HEADS = 8
A_KV_HEADS = 2
A_GROUP = A_HEADS // A_KV_HEADS
B_HEADS = 8
WINDOW = 128
D_FF = 4 * D_MODEL
EPS = 1e-6
LOG2E = 1.4426950408889634
NEG_BIG = -1e30

LANES = 128
HALF = LANES // 2

QA_BLK = 0
KA_BLK = 4
VA_BLK = 5
QB_BLK = 6
KB_BLK = 10
VB_BLK = 14
QKV_W = 18 * LANES
PROJ_W = QKV_W + LANES

PIECE_STRIDE = 8
ONE_LANE = 3 * PIECE_STRIDE

TOK_TILE = 512
Q_TILE = 512
K_TILE = 512
PRO_TILE = 512
SUB = 128


def _lane_iota(shape):
    return lax.broadcasted_iota(jnp.int32, shape, len(shape) - 1)


def _split3(v):
    hi = v.astype(BF16).astype(F32)
    r = v - hi
    mid = r.astype(BF16).astype(F32)
    lo = (r - mid).astype(BF16).astype(F32)
    return hi, mid, lo


def _head_rms_scale(t, e2):
    sq = t * t
    hi = sq.astype(BF16)
    lo = (sq - hi.astype(F32)).astype(BF16)
    ss = (jnp.dot(hi, e2, preferred_element_type=F32) + jnp.dot(lo, e2, preferred_element_type=F32))
    return lax.rsqrt(ss * (1.0 / HEAD_DIM) + EPS)


def _in_proj_kernel(x_ref, g_ref, w_ref, bf_ref, tri_ref, qkv_ref, cp_ref, carry_ref):
    @pl.when(pl.program_id(1) == 0)
    def _():
        carry_ref[...] = jnp.zeros_like(carry_ref)

    x = x_ref[0]
    ms = jnp.sum(x * x, axis=-1, keepdims=True) * (1.0 / D_MODEL)
    xn = (x * lax.rsqrt(ms + EPS) * g_ref[...]).astype(BF16)
    proj = jnp.dot(xn, w_ref[...], preferred_element_type=F32)
    qkv_ref[0] = proj[:, :QKV_W].astype(BF16)

    z = proj[:, QKV_W:] + bf_ref[...]
    lane = _lane_iota(z.shape)
    logf = jnp.minimum(z, 0.0) - jnp.log(1.0 + jnp.exp(-jnp.abs(z)))
    logf = jnp.where(lane < B_HEADS, logf * LOG2E, 0.0)

    hi, mid, lo = _split3(logf)
    packed = hi + pltpu.roll(mid, PIECE_STRIDE, 1) + pltpu.roll(lo, 2 * PIECE_STRIDE, 1)
    cs = jnp.dot(tri_ref[...], packed.astype(BF16), preferred_element_type=F32)
    cin = cs + pltpu.roll(cs, LANES - PIECE_STRIDE, 1) + pltpu.roll(cs, LANES - 2 * PIECE_STRIDE, 1)
    c = jnp.where(lane < B_HEADS, cin + carry_ref[...], 0.0)
    carry_ref[...] = c[c.shape[0] - 1:, :]

    chi, cmid, clo = _split3(c)
    pieces = (chi + pltpu.roll(cmid, PIECE_STRIDE, 1) + pltpu.roll(clo, 2 * PIECE_STRIDE, 1)
              + jnp.where(lane == ONE_LANE, 1.0, 0.0))
    cp_ref[0] = pieces.astype(BF16)


def _in_proj(x, g, w_pad, bf_pad, tri):
    b, s, d = x.shape
    tm = TOK_TILE
    const = lambda bi, si: (0, 0)
    return pl.pallas_call(
        _in_proj_kernel,
        grid=(b, s // tm),
        in_specs=[
            pl.BlockSpec((1, tm, d), lambda bi, si: (bi, si, 0)),
            pl.BlockSpec((1, d), const),
            pl.BlockSpec((d, PROJ_W), const),
            pl.BlockSpec((1, LANES), const),
            pl.BlockSpec((tm, tm), const),
        ],
        out_specs=[
            pl.BlockSpec((1, tm, QKV_W), lambda bi, si: (bi, si, 0)),
            pl.BlockSpec((1, tm, LANES), lambda bi, si: (bi, si, 0)),
        ],
        out_shape=[
            jax.ShapeDtypeStruct((b, s, QKV_W), BF16),
            jax.ShapeDtypeStruct((b, s, LANES), BF16),
        ],
        scratch_shapes=[pltpu.VMEM((1, LANES), F32)],
        compiler_params=pltpu.CompilerParams(
            dimension_semantics=("arbitrary", "arbitrary"),
            vmem_limit_bytes=48 * 1024 * 1024),
        name="in_proj",
    )(x, g, w_pad, bf_pad, tri)


def _swa_kernel(q_ref, kc_ref, kp_ref, vc_ref, vp_ref, gq_ref, gk_ref, e2_ref, bias_ref, sink_ref, o_ref):
    g = pl.program_id(1)
    qi = pl.program_id(2)
    tq = q_ref.shape[1]
    e2 = e2_ref[...]
    lane = _lane_iota((1, LANES))
    low = lane < HALF

    def kv_low(t):
        return jnp.where(g == 0, t, pltpu.roll(t, HALF, 1))

    k2 = jnp.concatenate([kp_ref[0], kc_ref[0]], axis=0).astype(F32)
    kn = k2 * _head_rms_scale(k2, e2) * gk_ref[...]
    k_lo = jnp.where(low, kv_low(kn), 0.0).astype(BF16)
    v2 = jnp.concatenate([vp_ref[0], vc_ref[0]], axis=0).astype(F32)
    v_lo = jnp.where(low, kv_low(v2), jnp.where(lane == HALF, 1.0, 0.0)).astype(BF16)

    q_heads = []
    for c in range(A_GROUP // 2):
        qc = q_ref[0, :, c * LANES:(c + 1) * LANES].astype(F32)
        qn = qc * _head_rms_scale(qc, e2) * gq_ref[...]
        q_heads.append(jnp.where(low, qn, 0.0).astype(BF16))
        q_heads.append(jnp.where(low, pltpu.roll(qn, HALF, 1), 0.0).astype(BF16))

    sink = sink_ref[0]
    outs = [[] for _ in range(A_GROUP)]
    for r in range(tq // SUB):
        q_st = jnp.concatenate([qh[r * SUB:(r + 1) * SUB] for qh in q_heads], axis=0)
        kw = k_lo[r * SUB:r * SUB + 2 * SUB]
        vw = v_lo[r * SUB:r * SUB + 2 * SUB]
        s = lax.dot_general(q_st, kw, (((1,), (1,)), ((), ())), preferred_element_type=F32)
        if r == 0:
            first = (qi == 0)
            s = s + jnp.where(first, bias_ref[0, 1], bias_ref[0, 0])
        else:
            s = s + bias_ref[0, 0]
        m = jnp.maximum(jnp.max(s, axis=-1, keepdims=True), sink[:, :1])
        p = jnp.exp(s - m)
        pv = jnp.dot(p.astype(BF16), vw, preferred_element_type=F32)
        denom = pv[:, HALF:HALF + 1] + jnp.exp(sink[:, :1] - m)
        o = pv * (1.0 / denom)
        for j in range(A_GROUP):
            outs[j].append(o[j * SUB:(j + 1) * SUB])
    heads = [jnp.concatenate(o, axis=0) for o in outs]
    for c in range(A_GROUP // 2):
        o_ref[0, :, c * LANES:(c + 1) * LANES] = jnp.where(
            low, heads[2 * c], pltpu.roll(heads[2 * c + 1], HALF, 1)).astype(o_ref.dtype)


def _swa(qkv, gq2, gk2, e2, bias, sink_rows):
    b, s, _ = qkv.shape
    tq = Q_TILE
    nsub = tq // SUB
    grp_w = A_GROUP * HEAD_DIM
    return pl.pallas_call(
        _swa_kernel,
        grid=(b, A_KV_HEADS, s // tq),
        in_specs=[
            pl.BlockSpec((1, tq, grp_w), lambda bi, g, qi: (bi, qi, g)),
            pl.BlockSpec((1, tq, LANES), lambda bi, g, qi: (bi, qi, KA_BLK)),
            pl.BlockSpec((1, SUB, LANES), lambda bi, g, qi: (bi, jnp.maximum(qi * nsub - 1, 0), KA_BLK)),
            pl.BlockSpec((1, tq, LANES), lambda bi, g, qi: (bi, qi, VA_BLK)),
            pl.BlockSpec((1, SUB, LANES), lambda bi, g, qi: (bi, jnp.maximum(qi * nsub - 1, 0), VA_BLK)),
            pl.BlockSpec((1, LANES), lambda bi, g, qi: (0, 0)),
            pl.BlockSpec((1, LANES), lambda bi, g, qi: (0, 0)),
            pl.BlockSpec((LANES, LANES), lambda bi, g, qi: (0, 0)),
            pl.BlockSpec((1, 2, A_GROUP * SUB, 2 * SUB), lambda bi, g, qi: (g, 0, 0, 0)),
            pl.BlockSpec((1, A_GROUP * SUB, LANES), lambda bi, g, qi: (g, 0, 0)),
        ],
        out_specs=pl.BlockSpec((1, tq, grp_w), lambda bi, g, qi: (bi, qi, g)),
        out_shape=jax.ShapeDtypeStruct((b, s, A_HEADS * HEAD_DIM), BF16),
        compiler_params=pltpu.CompilerParams(
            dimension_semantics=("arbitrary", "arbitrary", "arbitrary"),
            vmem_limit_bytes=48 * 1024 * 1024),
        name="swa",
    )(qkv, qkv, qkv, qkv, qkv, gq2, gk2, e2, bias, sink_rows)


def _fox_kernel(q_ref, k_ref, v_ref, cp_ref, selq_ref, selk_ref, gq_ref, gk_ref, e2_ref,
                o_ref, kaug_ref, qt_ref, vt_ref, m_ref, acc_ref, s_ref):
    qi = pl.program_id(2)
    tq = o_ref.shape[1]
    s_len = k_ref.shape[1]
    tk = K_TILE
    e2 = e2_ref[...]
    lane = _lane_iota((1, LANES))
    low = lane < HALF
    own = (low, jnp.logical_not(low))
    one_lane = (HALF, 0)

    @pl.when(qi == 0)
    def _prepare():
        def chunk(i, _):
            rows = pl.ds(pl.multiple_of(i * PRO_TILE, PRO_TILE), PRO_TILE)
            kc = k_ref[0, rows, :].astype(F32)
            kn = kc * _head_rms_scale(kc, e2) * gk_ref[...]
            qc = q_ref[0, rows, :].astype(F32)
            qn = qc * _head_rms_scale(qc, e2) * gq_ref[...]
            vc = v_ref[0, rows, :].astype(F32)
            cp = cp_ref[0, rows, :]
            for hh in range(2):
                extra_k = jnp.dot(cp, selk_ref[0, hh], preferred_element_type=F32)
                kaug_ref[hh, rows, :] = jnp.where(own[hh], kn, extra_k).astype(BF16)
                extra_q = jnp.dot(cp, selq_ref[0, hh], preferred_element_type=F32)
                qt_ref[hh, i] = jnp.where(own[hh], qn, extra_q).T.astype(BF16)
                va = jnp.where(own[hh], vc, jnp.where(lane == one_lane[hh], 1.0, 0.0))
                vt_ref[hh, i] = va.T.astype(BF16)
            return 0
        lax.fori_loop(0, s_len // PRO_TILE, chunk, 0)

    key = lax.broadcasted_iota(jnp.int32, (tk, tq), 0)
    qry = lax.broadcasted_iota(jnp.int32, (tk, tq), 1)
    causal = key <= qry

    m_ref[...] = jnp.full_like(m_ref, NEG_BIG)
    acc_ref[...] = jnp.zeros_like(acc_ref)

    def scores(hh, kb):
        rows = pl.ds(pl.multiple_of(kb * tk, tk), tk)
        return jnp.dot(kaug_ref[hh, rows, :], qt_ref[hh, qi], preferred_element_type=F32)

    def update(hh, s, kb, masked):
        if masked:
            s = jnp.where(causal, s, NEG_BIG)
        m_old = m_ref[hh]
        m_new = jnp.maximum(m_old, jnp.max(s, axis=0, keepdims=True))
        p = jnp.exp2(s - m_new).astype(BF16)
        acc_ref[hh] = (jnp.exp2(m_old - m_new) * acc_ref[hh]
                       + jnp.dot(vt_ref[hh, kb], p, preferred_element_type=F32))
        m_ref[hh] = m_new

    for hh in range(2):
        s_ref[hh] = scores(hh, 0)

    def body(kb, _):
        for hh in range(2):
            s = s_ref[hh]
            s_ref[hh] = scores(hh, kb + 1)
            update(hh, s, kb, False)
        return 0
    lax.fori_loop(0, qi, body, 0)

    results = []
    for hh in range(2):
        update(hh, s_ref[hh], qi, True)
        acc = acc_ref[hh]
        results.append(acc * (1.0 / acc[one_lane[hh]:one_lane[hh] + 1, :]))
    feat = lax.broadcasted_iota(jnp.int32, (LANES, 1), 0)
    o_ref[0] = jnp.where(feat < HALF, results[0], results[1]).T.astype(o_ref.dtype)


def _fox(qkv, cp, selq, selk, gq2, gk2, e2):
    b, s, _ = qkv.shape
    tq = Q_TILE
    assert tq == PRO_TILE and K_TILE == PRO_TILE
    npair = B_HEADS // 2
    nblk = s // PRO_TILE
    return pl.pallas_call(
        _fox_kernel,
        grid=(b, npair, s // tq),
        in_specs=[
            pl.BlockSpec((1, s, LANES), lambda bi, p, qi: (bi, 0, QB_BLK + p)),
            pl.BlockSpec((1, s, LANES), lambda bi, p, qi: (bi, 0, KB_BLK + p)),
            pl.BlockSpec((1, s, LANES), lambda bi, p, qi: (bi, 0, VB_BLK + p)),
            pl.BlockSpec((1, s, LANES), lambda bi, p, qi: (bi, 0, 0)),
            pl.BlockSpec((1, 2, LANES, LANES), lambda bi, p, qi: (p, 0, 0, 0)),
            pl.BlockSpec((1, 2, LANES, LANES), lambda bi, p, qi: (p, 0, 0, 0)),
            pl.BlockSpec((1, LANES), lambda bi, p, qi: (0, 0)),
            pl.BlockSpec((1, LANES), lambda bi, p, qi: (0, 0)),
            pl.BlockSpec((LANES, LANES), lambda bi, p, qi: (0, 0)),
        ],
        out_specs=pl.BlockSpec((1, tq, LANES), lambda bi, p, qi: (bi, qi, p)),
        out_shape=jax.ShapeDtypeStruct((b, s, B_HEADS * HEAD_DIM), BF16),
        scratch_shapes=[
            pltpu.VMEM((2, s, LANES), BF16),
            pltpu.VMEM((2, nblk, LANES, PRO_TILE), BF16),
            pltpu.VMEM((2, nblk, LANES, PRO_TILE), BF16),
            pltpu.VMEM((2, 1, tq), F32),
            pltpu.VMEM((2, LANES, tq), F32),
            pltpu.VMEM((2, K_TILE, tq), F32),
        ],
        compiler_params=pltpu.CompilerParams(
            dimension_semantics=("arbitrary", "arbitrary", "arbitrary"),
            vmem_limit_bytes=48 * 1024 * 1024),
        name="fox",
    )(qkv, qkv, qkv, cp, selq, selk, gq2, gk2, e2)


def _out_mlp_kernel(x_ref, oa_ref, ob_ref, woa_ref, wob_ref, g_ref, wup_ref, wdn_ref, y_ref):
    h = (x_ref[...]
         + jnp.dot(oa_ref[...], woa_ref[...], preferred_element_type=F32)
         + jnp.dot(ob_ref[...], wob_ref[...], preferred_element_type=F32))
    ms = jnp.sum(h * h, axis=-1, keepdims=True) * (1.0 / D_MODEL)
    hn = (h * lax.rsqrt(ms + EPS) * g_ref[...]).astype(BF16)
    u = jnp.maximum(jnp.dot(hn, wup_ref[...], preferred_element_type=F32), 0.0)
    y_ref[...] = h + jnp.dot((u * u).astype(BF16), wdn_ref[...], preferred_element_type=F32)


def _out_mlp(x2, oa2, ob2, woa, wob, g2, wup, wdn):
    n, d = x2.shape
    tm = TOK_TILE
    const = lambda i: (0, 0)
    resident = dict(pipeline_mode=pl.Buffered(1))
    return pl.pallas_call(
        _out_mlp_kernel,
        grid=(n // tm,),
        in_specs=[
            pl.BlockSpec((tm, d), lambda i: (i, 0)),
            pl.BlockSpec((tm, oa2.shape[1]), lambda i: (i, 0)),
            pl.BlockSpec((tm, ob2.shape[1]), lambda i: (i, 0)),
            pl.BlockSpec(woa.shape, const, **resident),
            pl.BlockSpec(wob.shape, const, **resident),
            pl.BlockSpec((1, d), const),
            pl.BlockSpec(wup.shape, const, **resident),
            pl.BlockSpec(wdn.shape, const, **resident),
        ],
        out_specs=pl.BlockSpec((tm, d), lambda i: (i, 0)),
        out_shape=jax.ShapeDtypeStruct((n, d), F32),
        compiler_params=pltpu.CompilerParams(
            dimension_semantics=("arbitrary",),
            vmem_limit_bytes=52 * 1024 * 1024),
        name="out_mlp",
    )(x2, oa2, ob2, woa, wob, g2, wup, wdn)


def _block_diag_ones():
    i = np.arange(LANES)
    return jnp.asarray((i[:, None] // HALF == i[None, :] // HALF).astype(np.float32), dtype=BF16)


def _fox_select_tables():
    selq = np.zeros((B_HEADS // 2, 2, LANES, LANES), np.float32)
    selk = np.zeros_like(selq)
    for p in range(B_HEADS // 2):
        for hh in range(2):
            h = 2 * p + hh
            base = HALF if hh == 0 else 0
            for piece in range(3):
                selq[p, hh, piece * PIECE_STRIDE + h, base + piece] = 1.0
                selq[p, hh, ONE_LANE, base + 3 + piece] = 1.0
                selk[p, hh, ONE_LANE, base + piece] = 1.0
                selk[p, hh, piece * PIECE_STRIDE + h, base + 3 + piece] = -1.0
    return jnp.asarray(selq, dtype=BF16), jnp.asarray(selk, dtype=BF16)


def _swa_bias_tables():
    slopes = np.exp2(-(8.0 / A_HEADS) * (np.arange(A_HEADS, dtype=np.float32) + 1.0)).astype(np.float32)
    qpos = SUB + np.arange(SUB)
    kpos = np.arange(2 * SUB)
    dist = qpos[:, None] - kpos[None, :]
    band = (dist >= 0) & (dist < WINDOW)
    out = np.zeros((A_KV_HEADS, 2, A_GROUP * SUB, 2 * SUB), np.float32)
    for g in range(A_KV_HEADS):
        for j in range(A_GROUP):
            alibi = (-slopes[g * A_GROUP + j] * dist.astype(np.float32)).astype(np.float32)
            rows = slice(j * SUB, (j + 1) * SUB)
            out[g, 0, rows] = np.where(band, alibi, NEG_BIG)
            out[g, 1, rows] = np.where(band & (kpos >= SUB)[None, :], alibi, NEG_BIG)
    return jnp.asarray(out)


def kernel(x, attn_norm_g, w_in, b_forget, q_norm_a, k_norm_a, sink_logits, q_norm_b, k_norm_b, w_out,
           mlp_norm_g, w_up, w_down):
    b, s, d = x.shape
    assert d == D_MODEL and s % Q_TILE == 0 and s % TOK_TILE == 0
    scale = 1.0 / math.sqrt(HEAD_DIM)

    w_pad = jnp.pad(w_in, ((0, 0), (0, PROJ_W - w_in.shape[1]))).astype(BF16)
    bf_pad = jnp.pad(b_forget.astype(F32), (0, LANES - B_HEADS)).reshape(1, LANES)
    tri = jnp.asarray(np.tril(np.ones((TOK_TILE, TOK_TILE), np.float32)), dtype=BF16)
    e2 = _block_diag_ones()
    selq, selk = _fox_select_tables()
    bias = _swa_bias_tables()
    sink_rows = jnp.broadcast_to(
        jnp.repeat(sink_logits.astype(F32).reshape(A_KV_HEADS, A_GROUP), SUB, axis=1)[:, :, None],
        (A_KV_HEADS, A_GROUP * SUB, LANES))
    gqa2 = (jnp.tile(q_norm_a.astype(F32), 2) * scale).reshape(1, LANES)
    gka2 = jnp.tile(k_norm_a.astype(F32), 2).reshape(1, LANES)
    gqb2 = (jnp.tile(q_norm_b.astype(F32), 2) * (scale * LOG2E)).reshape(1, LANES)
    gkb2 = jnp.tile(k_norm_b.astype(F32), 2).reshape(1, LANES)

    qkv, cp = _in_proj(x, attn_norm_g.astype(F32).reshape(1, d), w_pad, bf_pad, tri)
    out_a = _swa(qkv, gqa2, gka2, e2, bias, sink_rows)
    out_b = _fox(qkv, cp, selq, selk, gqb2, gkb2, e2)

    wo = w_out.astype(BF16)
    y = _out_mlp(x.reshape(b * s, d), out_a.reshape(b * s, -1), out_b.reshape(b * s, -1),
                 wo[:A_HEADS * HEAD_DIM], wo[A_HEADS * HEAD_DIM:],
                 mlp_norm_g.astype(F32).reshape(1, d), w_up.astype(BF16), w_down.astype(BF16))
    return y.reshape(b, s, d)
```

```python
import functools
import math

import numpy as np
import jax
import jax.numpy as jnp
from jax import lax
from jax.experimental import pallas as pl
from jax.experimental.pallas import tpu as pltpu

F32 = jnp.float32
BF16 = jnp.bfloat16

D_MODEL = 1024
HEAD_DIM = 64
A_HEADS = 8
A_KV_HEADS = 2
A_GROUP = A_HEADS // A_KV_HEADS
B_HEADS = 8
WINDOW = 128
D_FF = 4 * D_MODEL
EPS = 1e-6
LOG2E = 1.4426950408889634
NEG_BIG = -1e30

LANES = 128
HALF = LANES // 2

QA_BLK = 0
KA_BLK = 4
VA_BLK = 5
QB_BLK = 6
KB_BLK = 10
VB_BLK = 14
QKV_W = 18 * LANES
PROJ_W = QKV_W + LANES

PIECE_STRIDE = 8
ONE_LANE = 3 * PIECE_STRIDE

TOK_TILE = 512
Q_TILE = 512
PRO_TILE = 512
SUB = 128


def _lane_iota(shape):
    return lax.broadcasted_iota(jnp.int32, shape, len(shape) - 1)


def _split3(v):
    hi = v.astype(BF16).astype(F32)
    r = v - hi
    mid = r.astype(BF16).astype(F32)
    lo = (r - mid).astype(BF16).astype(F32)
    return hi, mid, lo


def _head_rms_scale(t, e2):
    sq = t * t
    hi = sq.astype(BF16)
    lo = (sq - hi.astype(F32)).astype(BF16)
    ss = (jnp.dot(hi, e2, preferred_element_type=F32) + jnp.dot(lo, e2, preferred_element_type=F32))
    return lax.rsqrt(ss * (1.0 / HEAD_DIM) + EPS)


def _in_proj_kernel(x_ref, g_ref, w_ref, bf_ref, tri_ref, qkv_ref, cp_ref, carry_ref):
    @pl.when(pl.program_id(1) == 0)
    def _():
        carry_ref[...] = jnp.zeros_like(carry_ref)

    x = x_ref[0]
    ms = jnp.sum(x * x, axis=-1, keepdims=True) * (1.0 / D_MODEL)
    xn = (x * lax.rsqrt(ms + EPS) * g_ref[...]).astype(BF16)
    proj = jnp.dot(xn, w_ref[...], preferred_element_type=F32)
    qkv_ref[0] = proj[:, :QKV_W].astype(BF16)

    z = proj[:, QKV_W:] + bf_ref[...]
    lane = _lane_iota(z.shape)
    logf = jnp.minimum(z, 0.0) - jnp.log(1.0 + jnp.exp(-jnp.abs(z)))
    logf = jnp.where(lane < B_HEADS, logf * LOG2E, 0.0)

    hi, mid, lo = _split3(logf)
    packed = hi + pltpu.roll(mid, PIECE_STRIDE, 1) + pltpu.roll(lo, 2 * PIECE_STRIDE, 1)
    cs = jnp.dot(tri_ref[...], packed.astype(BF16), preferred_element_type=F32)
    cin = cs + pltpu.roll(cs, LANES - PIECE_STRIDE, 1) + pltpu.roll(cs, LANES - 2 * PIECE_STRIDE, 1)
    c = jnp.where(lane < B_HEADS, cin + carry_ref[...], 0.0)
    carry_ref[...] = c[c.shape[0] - 1:, :]

    chi, cmid, clo = _split3(c)
    pieces = (chi + pltpu.roll(cmid, PIECE_STRIDE, 1) + pltpu.roll(clo, 2 * PIECE_STRIDE, 1)
              + jnp.where(lane == ONE_LANE, 1.0, 0.0))
    cp_ref[0] = pieces.astype(BF16)


def _in_proj(x, g, w_pad, bf_pad, tri):
    b, s, d = x.shape
    tm = TOK_TILE
    const = lambda bi, si: (0, 0)
    return pl.pallas_call(
        _in_proj_kernel,
        grid=(b, s // tm),
        in_specs=[
            pl.BlockSpec((1, tm, d), lambda bi, si: (bi, si, 0)),
            pl.BlockSpec((1, d), const),
            pl.BlockSpec((d, PROJ_W), const),
            pl.BlockSpec((1, LANES), const),
            pl.BlockSpec((tm, tm), const),
        ],
        out_specs=[
            pl.BlockSpec((1, tm, QKV_W), lambda bi, si: (bi, si, 0)),
            pl.BlockSpec((1, tm, LANES), lambda bi, si: (bi, si, 0)),
        ],
        out_shape=[
            jax.ShapeDtypeStruct((b, s, QKV_W), BF16),
            jax.ShapeDtypeStruct((b, s, LANES), BF16),
        ],
        scratch_shapes=[pltpu.VMEM((1, LANES), F32)],
        compiler_params=pltpu.CompilerParams(
            dimension_semantics=("arbitrary", "arbitrary"),
            vmem_limit_bytes=48 * 1024 * 1024),
        name="in_proj",
    )(x, g, w_pad, bf_pad, tri)


def _swa_kernel(q_ref, kc_ref, kp_ref, vc_ref, vp_ref, gq_ref, gk_ref, e2_ref, bias_ref, sink_ref, o_ref):
    qi = pl.program_id(1)
    tq = q_ref.shape[1]
    e2 = e2_ref[...]
    variant = jnp.where(qi == 0, 1, 0)
    feat = lax.broadcasted_iota(jnp.int32, (LANES, 1), 0)

    k2 = jnp.concatenate([kp_ref[0], kc_ref[0]], axis=0).astype(F32)
    kn = (k2 * _head_rms_scale(k2, e2) * gk_ref[...]).astype(BF16)
    v_t = jnp.concatenate([vp_ref[0], vc_ref[0]], axis=0).astype(F32).T
    one_row = (HALF, 0)
    v_ts = [jnp.where(feat == one_row[g], 1.0, v_t).astype(BF16) for g in range(A_KV_HEADS)]

    zeros = jnp.zeros((HALF, tq), F32)
    q_ts = []
    for c in range(A_HEADS // 2):
        g = c // (A_GROUP // 2)
        qc = q_ref[0, :, c * LANES:(c + 1) * LANES].astype(F32)
        qn_t = (qc * _head_rms_scale(qc, e2) * gq_ref[...]).T
        for half in range(2):
            f = qn_t[half * HALF:(half + 1) * HALF]
            q_ts.append(jnp.concatenate([f, zeros] if g == 0 else [zeros, f], axis=0).astype(BF16))

    out_heads = [[] for _ in range(A_HEADS)]
    for g in range(A_KV_HEADS):
        sink = sink_ref[g]
        for r in range(tq // SUB):
            keys = slice(r * SUB, r * SUB + 2 * SUB)
            q_t = jnp.concatenate([q_ts[A_GROUP * g + j][:, r * SUB:(r + 1) * SUB] for j in range(A_GROUP)], axis=1)
            s = jnp.dot(kn[keys], q_t, preferred_element_type=F32)
            s = s + (bias_ref[g, variant] if r == 0 else bias_ref[g, 0])
            m = jnp.maximum(jnp.max(s, axis=0, keepdims=True), sink)
            p = jnp.exp2(s - m).astype(BF16)
            pv = jnp.dot(v_ts[g][:, keys], p, preferred_element_type=F32)
            denom = pv[one_row[g]:one_row[g] + 1, :] + jnp.exp2(sink - m)
            o = pv[g * HALF:(g + 1) * HALF] * (1.0 / denom)
            for j in range(A_GROUP):
                out_heads[A_GROUP * g + j].append(o[:, j * SUB:(j + 1) * SUB])
    for c in range(A_HEADS // 2):
        pair_t = jnp.concatenate([jnp.concatenate(out_heads[2 * c], axis=1),
                                  jnp.concatenate(out_heads[2 * c + 1], axis=1)], axis=0)
        o_ref[0, :, c * LANES:(c + 1) * LANES] = pair_t.T.astype(o_ref.dtype)


def _swa(qkv, gq2, gk2, e2, bias_t, sink_rows):
    b, s, _ = qkv.shape
    tq = Q_TILE
    nsub = tq // SUB
    q_w = A_HEADS * HEAD_DIM
    const = lambda bi, qi: (0, 0)
    return pl.pallas_call(
        _swa_kernel,
        grid=(b, s // tq),
        in_specs=[
            pl.BlockSpec((1, tq, q_w), lambda bi, qi: (bi, qi, 0)),
            pl.BlockSpec((1, tq, LANES), lambda bi, qi: (bi, qi, KA_BLK)),
            pl.BlockSpec((1, SUB, LANES), lambda bi, qi: (bi, jnp.maximum(qi * nsub - 1, 0), KA_BLK)),
            pl.BlockSpec((1, tq, LANES), lambda bi, qi: (bi, qi, VA_BLK)),
            pl.BlockSpec((1, SUB, LANES), lambda bi, qi: (bi, jnp.maximum(qi * nsub - 1, 0), VA_BLK)),
            pl.BlockSpec((1, LANES), const),
            pl.BlockSpec((1, LANES), const),
            pl.BlockSpec((LANES, LANES), const),
            pl.BlockSpec(bias_t.shape, lambda bi, qi: (0, 0, 0, 0)),
            pl.BlockSpec(sink_rows.shape, lambda bi, qi: (0, 0, 0)),
        ],
        out_specs=pl.BlockSpec((1, tq, q_w), lambda bi, qi: (bi, qi, 0)),
        out_shape=jax.ShapeDtypeStruct((b, s, q_w), BF16),
        compiler_params=pltpu.CompilerParams(
            dimension_semantics=("arbitrary", "arbitrary"),
            vmem_limit_bytes=48 * 1024 * 1024),
        name="swa",
    )(qkv, qkv, qkv, qkv, qkv, gq2, gk2, e2, bias_t, sink_rows)


def _fox_kernel(q_ref, k_ref, v_ref, cp_ref, selq_ref, selk_ref, gq_ref, gk_ref, e2_ref,
                o_ref, kaug_ref, qt_ref, vt_ref, mask_ref, s_ref, smax_ref, m_ref, acc_ref):
    s_len = k_ref.shape[1]
    tq = tk = PRO_TILE
    n_tiles = s_len // tq
    n_steps = n_tiles * (n_tiles + 1) // 2
    e2 = e2_ref[...]
    lane = _lane_iota((1, LANES))
    low = lane < HALF
    own = (low, jnp.logical_not(low))
    one_lane = (HALF, 0)

    def prepare(i, _):
        rows = pl.ds(pl.multiple_of(i * PRO_TILE, PRO_TILE), PRO_TILE)
        kc = k_ref[0, rows, :].astype(F32)
        kn = kc * _head_rms_scale(kc, e2) * gk_ref[...]
        qc = q_ref[0, rows, :].astype(F32)
        qn = qc * _head_rms_scale(qc, e2) * gq_ref[...]
        vc = v_ref[0, rows, :].astype(F32)
        cp = cp_ref[0, rows, :]
        for hh in range(2):
            extra_k = jnp.dot(cp, selk_ref[0, hh], preferred_element_type=F32)
            kaug_ref[hh, rows, :] = jnp.where(own[hh], kn, extra_k).astype(BF16)
            extra_q = jnp.dot(cp, selq_ref[0, hh], preferred_element_type=F32)
            qt_ref[hh, i] = jnp.where(own[hh], qn, extra_q).T.astype(BF16)
            va = jnp.where(own[hh], vc, jnp.where(lane == one_lane[hh], 1.0, 0.0))
            vt_ref[hh, i] = va.T.astype(BF16)
        return 0
    lax.fori_loop(0, n_tiles, prepare, 0)

    key = lax.broadcasted_iota(jnp.int32, (tk, tq), 0)
    qry = lax.broadcasted_iota(jnp.int32, (tk, tq), 1)
    mask_ref[0] = jnp.zeros((tk, tq), F32)
    mask_ref[1] = jnp.where(key <= qry, 0.0, NEG_BIG)
    s_ref[...] = jnp.zeros_like(s_ref)
    smax_ref[...] = jnp.zeros_like(smax_ref)
    m_ref[...] = jnp.zeros_like(m_ref)
    acc_ref[...] = jnp.ones_like(acc_ref)

    def iteration(t, carry):
        qi0, kb0, qi1, kb1 = carry
        diag = jnp.where(kb0 == qi0, 1, 0)
        krows = pl.ds(pl.multiple_of(kb0 * tk, tk), tk)
        for hh in range(2):
            s = s_ref[hh]
            smax = smax_ref[hh]
            nxt = jnp.dot(kaug_ref[hh, krows, :], qt_ref[hh, qi0], preferred_element_type=F32) + mask_ref[diag]
            s_ref[hh] = nxt
            smax_ref[hh] = jnp.max(nxt, axis=0, keepdims=True)
            m_old = jnp.where(kb1 == 0, NEG_BIG, m_ref[hh])
            m_new = jnp.maximum(m_old, smax)
            p = jnp.exp2(s - m_new).astype(BF16)
            acc_ref[hh, qi1] = (jnp.exp2(m_old - m_new) * acc_ref[hh, qi1]
                                + jnp.dot(vt_ref[hh, kb1], p, preferred_element_type=F32))
            m_ref[hh] = m_new
        wrap = kb0 == qi0
        last = jnp.logical_and(wrap, qi0 == n_tiles - 1)
        qi_n = jnp.where(jnp.logical_and(wrap, jnp.logical_not(last)), qi0 + 1, qi0)
        kb_n = jnp.where(last, kb0, jnp.where(wrap, 0, kb0 + 1))
        return qi_n, kb_n, qi0, kb0

    zero = jnp.int32(0)
    lax.fori_loop(0, n_steps + 1, iteration, (zero, zero, zero, zero))

    feat = lax.broadcasted_iota(jnp.int32, (LANES, 1), 0)

    def finish(i, _):
        outs = []
        for hh in range(2):
            acc = acc_ref[hh, i]
            outs.append(acc * (1.0 / acc[one_lane[hh]:one_lane[hh] + 1, :]))
        rows = pl.ds(pl.multiple_of(i * tq, tq), tq)
        o_ref[0, rows, :] = jnp.where(feat < HALF, outs[0], outs[1]).T.astype(o_ref.dtype)
        return 0
    lax.fori_loop(0, n_tiles, finish, 0)


def _fox(qkv, cp, selq, selk, gq2, gk2, e2):
    b, s, _ = qkv.shape
    assert s % PRO_TILE == 0
    npair = B_HEADS // 2
    nblk = s // PRO_TILE
    const = lambda bi, p: (0, 0)
    return pl.pallas_call(
        _fox_kernel,
        grid=(b, npair),
        in_specs=[
            pl.BlockSpec((1, s, LANES), lambda bi, p: (bi, 0, QB_BLK + p)),
            pl.BlockSpec((1, s, LANES), lambda bi, p: (bi, 0, KB_BLK + p)),
            pl.BlockSpec((1, s, LANES), lambda bi, p: (bi, 0, VB_BLK + p)),
            pl.BlockSpec((1, s, LANES), lambda bi, p: (bi, 0, 0)),
            pl.BlockSpec((1, 2, LANES, LANES), lambda bi, p: (p, 0, 0, 0)),
            pl.BlockSpec((1, 2, LANES, LANES), lambda bi, p: (p, 0, 0, 0)),
            pl.BlockSpec((1, LANES), const),
            pl.BlockSpec((1, LANES), const),
            pl.BlockSpec((LANES, LANES), const),
        ],
        out_specs=pl.BlockSpec((1, s, LANES), lambda bi, p: (bi, 0, p)),
        out_shape=jax.ShapeDtypeStruct((b, s, B_HEADS * HEAD_DIM), BF16),
        scratch_shapes=[
            pltpu.VMEM((2, s, LANES), BF16),
            pltpu.VMEM((2, nblk, LANES, PRO_TILE), BF16),
            pltpu.VMEM((2, nblk, LANES, PRO_TILE), BF16),
            pltpu.VMEM((2, PRO_TILE, PRO_TILE), F32),
            pltpu.VMEM((2, PRO_TILE, PRO_TILE), F32),
            pltpu.VMEM((2, 1, PRO_TILE), F32),
            pltpu.VMEM((2, 1, PRO_TILE), F32),
            pltpu.VMEM((2, nblk, LANES, PRO_TILE), F32),
        ],
        compiler_params=pltpu.CompilerParams(
            dimension_semantics=("arbitrary", "arbitrary"),
            vmem_limit_bytes=48 * 1024 * 1024),
        name="fox",
    )(qkv, qkv, qkv, cp, selq, selk, gq2, gk2, e2)


def _out_mlp_kernel(x_ref, oa_ref, ob_ref, woa_ref, wob_ref, g_ref, wup_ref, wdn_ref, y_ref):
    h = (x_ref[...]
         + jnp.dot(oa_ref[...], woa_ref[...], preferred_element_type=F32)
         + jnp.dot(ob_ref[...], wob_ref[...], preferred_element_type=F32))
    ms = jnp.sum(h * h, axis=-1, keepdims=True) * (1.0 / D_MODEL)
    hn = (h * lax.rsqrt(ms + EPS) * g_ref[...]).astype(BF16)
    u = jnp.maximum(jnp.dot(hn, wup_ref[...], preferred_element_type=F32), 0.0)
    y_ref[...] = h + jnp.dot((u * u).astype(BF16), wdn_ref[...], preferred_element_type=F32)


def _out_mlp(x2, oa2, ob2, woa, wob, g2, wup, wdn):
    n, d = x2.shape
    tm = TOK_TILE
    const = lambda i: (0, 0)
    resident = dict(pipeline_mode=pl.Buffered(1))
    return pl.pallas_call(
        _out_mlp_kernel,
        grid=(n // tm,),
        in_specs=[
            pl.BlockSpec((tm, d), lambda i: (i, 0)),
            pl.BlockSpec((tm, oa2.shape[1]), lambda i: (i, 0)),
            pl.BlockSpec((tm, ob2.shape[1]), lambda i: (i, 0)),
            pl.BlockSpec(woa.shape, const, **resident),
            pl.BlockSpec(wob.shape, const, **resident),
            pl.BlockSpec((1, d), const),
            pl.BlockSpec(wup.shape, const, **resident),
            pl.BlockSpec(wdn.shape, const, **resident),
        ],
        out_specs=pl.BlockSpec((tm, d), lambda i: (i, 0)),
        out_shape=jax.ShapeDtypeStruct((n, d), F32),
        compiler_params=pltpu.CompilerParams(
            dimension_semantics=("arbitrary",),
            vmem_limit_bytes=52 * 1024 * 1024),
        name="out_mlp",
    )(x2, oa2, ob2, woa, wob, g2, wup, wdn)


def _block_diag_ones():
    i = np.arange(LANES)
    return jnp.asarray((i[:, None] // HALF == i[None, :] // HALF).astype(np.float32), dtype=BF16)


def _fox_select_tables():
    selq = np.zeros((B_HEADS // 2, 2, LANES, LANES), np.float32)
    selk = np.zeros_like(selq)
    for p in range(B_HEADS // 2):
        for hh in range(2):
            h = 2 * p + hh
            base = HALF if hh == 0 else 0
            for piece in range(3):
                selq[p, hh, piece * PIECE_STRIDE + h, base + piece] = 1.0
                selq[p, hh, ONE_LANE, base + 3 + piece] = 1.0
                selk[p, hh, ONE_LANE, base + piece] = 1.0
                selk[p, hh, piece * PIECE_STRIDE + h, base + 3 + piece] = -1.0
    return jnp.asarray(selq, dtype=BF16), jnp.asarray(selk, dtype=BF16)


def _swa_bias_tables():
    slopes = np.exp2(-(8.0 / A_HEADS) * (np.arange(A_HEADS, dtype=np.float32) + 1.0)).astype(np.float32)
    qpos = SUB + np.arange(SUB)
    kpos = np.arange(2 * SUB)
    dist = kpos[:, None] * 0 + qpos[None, :] - kpos[:, None]
    band = (dist >= 0) & (dist < WINDOW)
    out = np.zeros((A_KV_HEADS, 2, 2 * SUB, A_GROUP * SUB), np.float32)
    for g in range(A_KV_HEADS):
        for j in range(A_GROUP):
            alibi = (-slopes[g * A_GROUP + j] * dist.astype(np.float32)).astype(np.float32) * np.float32(LOG2E)
            cols = slice(j * SUB, (j + 1) * SUB)
            out[g, 0, :, cols] = np.where(band, alibi, NEG_BIG)
            out[g, 1, :, cols] = np.where(band & (kpos >= SUB)[:, None], alibi, NEG_BIG)
    return jnp.asarray(out)


def kernel(x, attn_norm_g, w_in, b_forget, q_norm_a, k_norm_a, sink_logits, q_norm_b, k_norm_b, w_out,
           mlp_norm_g, w_up, w_down):
    b, s, d = x.shape
    assert d == D_MODEL and s % Q_TILE == 0 and s % TOK_TILE == 0
    scale = 1.0 / math.sqrt(HEAD_DIM)

    w_pad = jnp.pad(w_in, ((0, 0), (0, PROJ_W - w_in.shape[1]))).astype(BF16)
    bf_pad = jnp.pad(b_forget.astype(F32), (0, LANES - B_HEADS)).reshape(1, LANES)
    tri = jnp.asarray(np.tril(np.ones((TOK_TILE, TOK_TILE), np.float32)), dtype=BF16)
    e2 = _block_diag_ones()
    selq, selk = _fox_select_tables()
    bias = _swa_bias_tables()
    sink_rows = jnp.repeat(sink_logits.astype(F32).reshape(A_KV_HEADS, 1, A_GROUP) * LOG2E, SUB, axis=2)
    gqa2 = (jnp.tile(q_norm_a.astype(F32), 2) * (scale * LOG2E)).reshape(1, LANES)
    gka2 = jnp.tile(k_norm_a.astype(F32), 2).reshape(1, LANES)
    gqb2 = (jnp.tile(q_norm_b.astype(F32), 2) * (scale * LOG2E)).reshape(1, LANES)
    gkb2 = jnp.tile(k_norm_b.astype(F32), 2).reshape(1, LANES)

    qkv, cp = _in_proj(x, attn_norm_g.astype(F32).reshape(1, d), w_pad, bf_pad, tri)
    out_a = _swa(qkv, gqa2, gka2, e2, bias, sink_rows)
    out_b = _fox(qkv, cp, selq, selk, gqb2, gkb2, e2)

    wo = w_out.astype(BF16)
    y = _out_mlp(x.reshape(b * s, d), out_a.reshape(b * s, -1), out_b.reshape(b * s, -1),
                 wo[:A_HEADS * HEAD_DIM], wo[A_HEADS * HEAD_DIM:],
                 mlp_norm_g.astype(F32).reshape(1, d), w_up.astype(BF16), w_down.astype(BF16))
    return y.reshape(b, s, d)
```

```python
import functools
import math

import numpy as np
import jax
import jax.numpy as jnp
from jax import lax
from jax.experimental import pallas as pl
from jax.experimental.pallas import tpu as pltpu

F32 = jnp.float32
BF16 = jnp.bfloat16

D_MODEL = 1024
HEAD_DIM = 64
A_HEADS = 8
A_KV_HEADS = 2
A_GROUP = A_HEADS // A_KV_HEADS
B_HEADS = 8
WINDOW = 128
D_FF = 4 * D_MODEL
EPS = 1e-6
LOG2E = 1.4426950408889634
NEG_BIG = -1e30

LANES = 128
HALF = LANES // 2

QA_BLK = 0
KA_BLK = 4
VA_BLK = 5
QB_BLK = 6
KB_BLK = 10
VB_BLK = 14
QKV_W = 18 * LANES
PROJ_W = QKV_W + LANES

PIECE_STRIDE = 8
ONE_LANE = 3 * PIECE_STRIDE

TOK_TILE = 512
Q_TILE = 512
PRO_TILE = 512
FOX_STEPS_PER_TRIP = 2
SUB = 128


def _lane_iota(shape):
    return lax.broadcasted_iota(jnp.int32, shape, len(shape) - 1)


def _split3(v):
    hi = v.astype(BF16).astype(F32)
    r = v - hi
    mid = r.astype(BF16).astype(F32)
    lo = (r - mid).astype(BF16).astype(F32)
    return hi, mid, lo


def _head_rms_scale(t, e2):
    sq = t * t
    hi = sq.astype(BF16)
    lo = (sq - hi.astype(F32)).astype(BF16)
    ss = (jnp.dot(hi, e2, preferred_element_type=F32) + jnp.dot(lo, e2, preferred_element_type=F32))
    return lax.rsqrt(ss * (1.0 / HEAD_DIM) + EPS)


def _in_proj_kernel(x_ref, g_ref, w_ref, bf_ref, tri_ref, qkv_ref, cp_ref, carry_ref):
    @pl.when(pl.program_id(1) == 0)
    def _():
        carry_ref[...] = jnp.zeros_like(carry_ref)

    tm = x_ref.shape[1]
    halves = [slice(h * (tm // 2), (h + 1) * (tm // 2)) for h in range(2)]
    xn = []
    for rows in halves:
        x = x_ref[0, rows, :]
        ms = jnp.sum(x * x, axis=-1, keepdims=True) * (1.0 / D_MODEL)
        xn.append((x * lax.rsqrt(ms + EPS) * g_ref[...]).astype(BF16))
    z = jnp.concatenate([jnp.dot(h, w_ref[:, QKV_W:], preferred_element_type=F32) for h in xn], axis=0)
    for rows, h in zip(halves, xn):
        qkv_ref[0, rows, :] = jnp.dot(h, w_ref[:, :QKV_W], preferred_element_type=F32).astype(BF16)

    z = z + bf_ref[...]
    lane = _lane_iota(z.shape)
    logf = jnp.minimum(z, 0.0) - jnp.log(1.0 + jnp.exp(-jnp.abs(z)))
    logf = jnp.where(lane < B_HEADS, logf * LOG2E, 0.0)

    hi, mid, lo = _split3(logf)
    packed = hi + pltpu.roll(mid, PIECE_STRIDE, 1) + pltpu.roll(lo, 2 * PIECE_STRIDE, 1)
    cs = jnp.dot(tri_ref[...], packed.astype(BF16), preferred_element_type=F32)
    cin = cs + pltpu.roll(cs, LANES - PIECE_STRIDE, 1) + pltpu.roll(cs, LANES - 2 * PIECE_STRIDE, 1)
    c = jnp.where(lane < B_HEADS, cin + carry_ref[...], 0.0)
    carry_ref[...] = c[c.shape[0] - 1:, :]

    chi, cmid, clo = _split3(c)
    pieces = (chi + pltpu.roll(cmid, PIECE_STRIDE, 1) + pltpu.roll(clo, 2 * PIECE_STRIDE, 1)
              + jnp.where(lane == ONE_LANE, 1.0, 0.0))
    cp_ref[0] = pieces.astype(BF16)


def _in_proj(x, g, w_pad, bf_pad, tri):
    b, s, d = x.shape
    tm = TOK_TILE
    const = lambda bi, si: (0, 0)
    return pl.pallas_call(
        _in_proj_kernel,
        grid=(b, s // tm),
        in_specs=[
            pl.BlockSpec((1, tm, d), lambda bi, si: (bi, si, 0)),
            pl.BlockSpec((1, d), const),
            pl.BlockSpec((d, PROJ_W), const),
            pl.BlockSpec((1, LANES), const),
            pl.BlockSpec((tm, tm), const),
        ],
        out_specs=[
            pl.BlockSpec((1, tm, QKV_W), lambda bi, si: (bi, si, 0)),
            pl.BlockSpec((1, tm, LANES), lambda bi, si: (bi, si, 0)),
        ],
        out_shape=[
            jax.ShapeDtypeStruct((b, s, QKV_W), BF16),
            jax.ShapeDtypeStruct((b, s, LANES), BF16),
        ],
        scratch_shapes=[pltpu.VMEM((1, LANES), F32)],
        compiler_params=pltpu.CompilerParams(
            dimension_semantics=("arbitrary", "arbitrary"),
            vmem_limit_bytes=48 * 1024 * 1024),
        name="in_proj",
    )(x, g, w_pad, bf_pad, tri)


def _swa_kernel(q_ref, kc_ref, kp_ref, vc_ref, vp_ref, gq_ref, gk_ref, e2_ref, bias_ref, sink_ref, o_ref):
    qi = pl.program_id(1)
    tq = q_ref.shape[1]
    e2 = e2_ref[...]
    variant = jnp.where(qi == 0, 1, 0)
    feat = lax.broadcasted_iota(jnp.int32, (LANES, 1), 0)

    k2 = jnp.concatenate([kp_ref[0], kc_ref[0]], axis=0).astype(F32)
    kn = (k2 * _head_rms_scale(k2, e2) * gk_ref[...]).astype(BF16)
    v_t = jnp.concatenate([vp_ref[0], vc_ref[0]], axis=0).astype(F32).T
    one_row = (HALF, 0)
    v_ts = [jnp.where(feat == one_row[g], 1.0, v_t).astype(BF16) for g in range(A_KV_HEADS)]

    zeros = jnp.zeros((HALF, tq), F32)
    q_ts = []
    for c in range(A_HEADS // 2):
        g = c // (A_GROUP // 2)
        qc = q_ref[0, :, c * LANES:(c + 1) * LANES].astype(F32)
        qn_t = (qc * _head_rms_scale(qc, e2) * gq_ref[...]).T
        for half in range(2):
            f = qn_t[half * HALF:(half + 1) * HALF]
            q_ts.append(jnp.concatenate([f, zeros] if g == 0 else [zeros, f], axis=0).astype(BF16))

    out_heads = [[] for _ in range(A_HEADS)]
    for g in range(A_KV_HEADS):
        sink = sink_ref[g]
        for r in range(tq // SUB):
            keys = slice(r * SUB, r * SUB + 2 * SUB)
            q_t = jnp.concatenate([q_ts[A_GROUP * g + j][:, r * SUB:(r + 1) * SUB] for j in range(A_GROUP)], axis=1)
            s = jnp.dot(kn[keys], q_t, preferred_element_type=F32)
            s = s + (bias_ref[g, variant] if r == 0 else bias_ref[g, 0])
            m = jnp.maximum(jnp.max(s, axis=0, keepdims=True), sink)
            p = jnp.exp2(s - m).astype(BF16)
            pv = jnp.dot(v_ts[g][:, keys], p, preferred_element_type=F32)
            denom = pv[one_row[g]:one_row[g] + 1, :] + jnp.exp2(sink - m)
            o = pv[g * HALF:(g + 1) * HALF] * (1.0 / denom)
            for j in range(A_GROUP):
                out_heads[A_GROUP * g + j].append(o[:, j * SUB:(j + 1) * SUB])
    for c in range(A_HEADS // 2):
        pair_t = jnp.concatenate([jnp.concatenate(out_heads[2 * c], axis=1),
                                  jnp.concatenate(out_heads[2 * c + 1], axis=1)], axis=0)
        o_ref[0, :, c * LANES:(c + 1) * LANES] = pair_t.T.astype(o_ref.dtype)


def _swa(qkv, gq2, gk2, e2, bias_t, sink_rows):
    b, s, _ = qkv.shape
    tq = Q_TILE
    nsub = tq // SUB
    q_w = A_HEADS * HEAD_DIM
    const = lambda bi, qi: (0, 0)
    return pl.pallas_call(
        _swa_kernel,
        grid=(b, s // tq),
        in_specs=[
            pl.BlockSpec((1, tq, q_w), lambda bi, qi: (bi, qi, 0)),
            pl.BlockSpec((1, tq, LANES), lambda bi, qi: (bi, qi, KA_BLK)),
            pl.BlockSpec((1, SUB, LANES), lambda bi, qi: (bi, jnp.maximum(qi * nsub - 1, 0), KA_BLK)),
            pl.BlockSpec((1, tq, LANES), lambda bi, qi: (bi, qi, VA_BLK)),
            pl.BlockSpec((1, SUB, LANES), lambda bi, qi: (bi, jnp.maximum(qi * nsub - 1, 0), VA_BLK)),
            pl.BlockSpec((1, LANES), const),
            pl.BlockSpec((1, LANES), const),
            pl.BlockSpec((LANES, LANES), const),
            pl.BlockSpec(bias_t.shape, lambda bi, qi: (0, 0, 0, 0)),
            pl.BlockSpec(sink_rows.shape, lambda bi, qi: (0, 0, 0)),
        ],
        out_specs=pl.BlockSpec((1, tq, q_w), lambda bi, qi: (bi, qi, 0)),
        out_shape=jax.ShapeDtypeStruct((b, s, q_w), BF16),
        compiler_params=pltpu.CompilerParams(
            dimension_semantics=("arbitrary", "arbitrary"),
            vmem_limit_bytes=48 * 1024 * 1024),
        name="swa",
    )(qkv, qkv, qkv, qkv, qkv, gq2, gk2, e2, bias_t, sink_rows)


def _fox_kernel(q_ref, k_ref, v_ref, cp_ref, selq_ref, selk_ref, gq_ref, gk_ref, e2_ref,
                o_ref, kaug_ref, qt_ref, vt_ref, mask_ref, s_ref, smax_ref, m_ref, acc_ref):
    s_len = k_ref.shape[1]
    tq = tk = PRO_TILE
    n_tiles = s_len // tq
    n_steps = n_tiles * (n_tiles + 1) // 2
    e2 = e2_ref[...]
    lane = _lane_iota((1, LANES))
    low = lane < HALF
    own = (low, jnp.logical_not(low))
    one_lane = (HALF, 0)

    def prepare(i, _):
        rows = pl.ds(pl.multiple_of(i * PRO_TILE, PRO_TILE), PRO_TILE)
        kc = k_ref[0, rows, :].astype(F32)
        kn = kc * _head_rms_scale(kc, e2) * gk_ref[...]
        qc = q_ref[0, rows, :].astype(F32)
        qn = qc * _head_rms_scale(qc, e2) * gq_ref[...]
        vc = v_ref[0, rows, :].astype(F32)
        cp = cp_ref[0, rows, :]
        for hh in range(2):
            extra_k = jnp.dot(cp, selk_ref[0, hh], preferred_element_type=F32)
            kaug_ref[hh, rows, :] = jnp.where(own[hh], kn, extra_k).astype(BF16)
            extra_q = jnp.dot(cp, selq_ref[0, hh], preferred_element_type=F32)
            qt_ref[hh, i] = jnp.where(own[hh], qn, extra_q).T.astype(BF16)
            va = jnp.where(own[hh], vc, jnp.where(lane == one_lane[hh], 1.0, 0.0))
            vt_ref[hh, i] = va.T.astype(BF16)
        return 0
    lax.fori_loop(0, n_tiles, prepare, 0, unroll=2)

    key = lax.broadcasted_iota(jnp.int32, (tk, tq), 0)
    qry = lax.broadcasted_iota(jnp.int32, (tk, tq), 1)
    mask_ref[0] = jnp.zeros((tk, tq), F32)
    mask_ref[1] = jnp.where(key <= qry, 0.0, NEG_BIG)
    s_ref[...] = jnp.zeros_like(s_ref)
    smax_ref[...] = jnp.zeros_like(smax_ref)
    m_ref[...] = jnp.zeros_like(m_ref)
    acc_ref[...] = jnp.ones_like(acc_ref)

    def issue_scores(hh, qi0, kb0):
        diag = jnp.where(kb0 == qi0, 1, 0)
        krows = pl.ds(pl.multiple_of(kb0 * tk, tk), tk)
        nxt = jnp.dot(kaug_ref[hh, krows, :], qt_ref[hh, qi0], preferred_element_type=F32) + mask_ref[diag]
        s_ref[hh] = nxt
        smax_ref[hh] = jnp.max(nxt, axis=0, keepdims=True)

    def iteration(carry):
        qi0, kb0, qi1, kb1 = carry
        for hh in range(2):
            s = s_ref[hh]
            smax = smax_ref[hh]
            issue_scores(hh, qi0, kb0)
            m_old = jnp.where(kb1 == 0, NEG_BIG, m_ref[hh])
            m_new = jnp.maximum(m_old, smax)
            p = jnp.exp2(s - m_new).astype(BF16)
            acc_ref[hh, qi1] = (jnp.exp2(m_old - m_new) * acc_ref[hh, qi1]
                                + jnp.dot(vt_ref[hh, kb1], p, preferred_element_type=F32))
            m_ref[hh] = m_new
        wrap = kb0 == qi0
        last = jnp.logical_and(wrap, qi0 == n_tiles - 1)
        qi_n = jnp.where(jnp.logical_and(wrap, jnp.logical_not(last)), qi0 + 1, qi0)
        kb_n = jnp.where(last, kb0, jnp.where(wrap, 0, kb0 + 1))
        return qi_n, kb_n, qi0, kb0

    zero = jnp.int32(0)
    for hh in range(2):
        issue_scores(hh, zero, zero)
    first_next = (jnp.int32(1), zero) if n_tiles > 1 else (zero, zero)
    def trip(_, carry):
        for _ in range(FOX_STEPS_PER_TRIP):
            carry = iteration(carry)
        return carry
    carry = lax.fori_loop(0, n_steps // FOX_STEPS_PER_TRIP, trip, (*first_next, zero, zero))
    for _ in range(n_steps % FOX_STEPS_PER_TRIP):
        carry = iteration(carry)

    feat = lax.broadcasted_iota(jnp.int32, (LANES, 1), 0)

    def finish(i, _):
        outs = []
        for hh in range(2):
            acc = acc_ref[hh, i]
            outs.append(acc * (1.0 / acc[one_lane[hh]:one_lane[hh] + 1, :]))
        rows = pl.ds(pl.multiple_of(i * tq, tq), tq)
        o_ref[0, rows, :] = jnp.where(feat < HALF, outs[0], outs[1]).T.astype(o_ref.dtype)
        return 0
    lax.fori_loop(0, n_tiles, finish, 0, unroll=2)


def _fox(qkv, cp, selq, selk, gq2, gk2, e2):
    b, s, _ = qkv.shape
    assert s % PRO_TILE == 0
    npair = B_HEADS // 2
    nblk = s // PRO_TILE
    const = lambda bi, p: (0, 0)
    return pl.pallas_call(
        _fox_kernel,
        grid=(b, npair),
        in_specs=[
            pl.BlockSpec((1, s, LANES), lambda bi, p: (bi, 0, QB_BLK + p)),
            pl.BlockSpec((1, s, LANES), lambda bi, p: (bi, 0, KB_BLK + p)),
            pl.BlockSpec((1, s, LANES), lambda bi, p: (bi, 0, VB_BLK + p)),
            pl.BlockSpec((1, s, LANES), lambda bi, p: (bi, 0, 0)),
            pl.BlockSpec((1, 2, LANES, LANES), lambda bi, p: (p, 0, 0, 0)),
            pl.BlockSpec((1, 2, LANES, LANES), lambda bi, p: (p, 0, 0, 0)),
            pl.BlockSpec((1, LANES), const),
            pl.BlockSpec((1, LANES), const),
            pl.BlockSpec((LANES, LANES), const),
        ],
        out_specs=pl.BlockSpec((1, s, LANES), lambda bi, p: (bi, 0, p)),
        out_shape=jax.ShapeDtypeStruct((b, s, B_HEADS * HEAD_DIM), BF16),
        scratch_shapes=[
            pltpu.VMEM((2, s, LANES), BF16),
            pltpu.VMEM((2, nblk, LANES, PRO_TILE), BF16),
            pltpu.VMEM((2, nblk, LANES, PRO_TILE), BF16),
            pltpu.VMEM((2, PRO_TILE, PRO_TILE), F32),
            pltpu.VMEM((2, PRO_TILE, PRO_TILE), F32),
            pltpu.VMEM((2, 1, PRO_TILE), F32),
            pltpu.VMEM((2, 1, PRO_TILE), F32),
            pltpu.VMEM((2, nblk, LANES, PRO_TILE), F32),
        ],
        compiler_params=pltpu.CompilerParams(
            dimension_semantics=("arbitrary", "arbitrary"),
            vmem_limit_bytes=48 * 1024 * 1024),
        name="fox",
    )(qkv, qkv, qkv, cp, selq, selk, gq2, gk2, e2)


def _out_mlp_kernel(x_ref, oa_ref, ob_ref, woa_ref, wob_ref, g_ref, wup_ref, wdn_ref, y_ref):
    h = (x_ref[...]
         + jnp.dot(oa_ref[...], woa_ref[...], preferred_element_type=F32)
         + jnp.dot(ob_ref[...], wob_ref[...], preferred_element_type=F32))
    ms = jnp.sum(h * h, axis=-1, keepdims=True) * (1.0 / D_MODEL)
    hn = (h * lax.rsqrt(ms + EPS) * g_ref[...]).astype(BF16)
    u = jnp.maximum(jnp.dot(hn, wup_ref[...], preferred_element_type=F32), 0.0)
    y_ref[...] = h + jnp.dot((u * u).astype(BF16), wdn_ref[...], preferred_element_type=F32)


def _out_mlp(x2, oa2, ob2, woa, wob, g2, wup, wdn):
    n, d = x2.shape
    tm = TOK_TILE
    const = lambda i: (0, 0)
    resident = dict(pipeline_mode=pl.Buffered(1))
    return pl.pallas_call(
        _out_mlp_kernel,
        grid=(n // tm,),
        in_specs=[
            pl.BlockSpec((tm, d), lambda i: (i, 0)),
            pl.BlockSpec((tm, oa2.shape[1]), lambda i: (i, 0)),
            pl.BlockSpec((tm, ob2.shape[1]), lambda i: (i, 0)),
            pl.BlockSpec(woa.shape, const, **resident),
            pl.BlockSpec(wob.shape, const, **resident),
            pl.BlockSpec((1, d), const),
            pl.BlockSpec(wup.shape, const, **resident),
            pl.BlockSpec(wdn.shape, const, **resident),
        ],
        out_specs=pl.BlockSpec((tm, d), lambda i: (i, 0)),
        out_shape=jax.ShapeDtypeStruct((n, d), F32),
        compiler_params=pltpu.CompilerParams(
            dimension_semantics=("arbitrary",),
            vmem_limit_bytes=52 * 1024 * 1024),
        name="out_mlp",
    )(x2, oa2, ob2, woa, wob, g2, wup, wdn)


def _block_diag_ones():
    i = np.arange(LANES)
    return jnp.asarray((i[:, None] // HALF == i[None, :] // HALF).astype(np.float32), dtype=BF16)


def _fox_select_tables():
    selq = np.zeros((B_HEADS // 2, 2, LANES, LANES), np.float32)
    selk = np.zeros_like(selq)
    for p in range(B_HEADS // 2):
        for hh in range(2):
            h = 2 * p + hh
            base = HALF if hh == 0 else 0
            for piece in range(3):
                selq[p, hh, piece * PIECE_STRIDE + h, base + piece] = 1.0
                selq[p, hh, ONE_LANE, base + 3 + piece] = 1.0
                selk[p, hh, ONE_LANE, base + piece] = 1.0
                selk[p, hh, piece * PIECE_STRIDE + h, base + 3 + piece] = -1.0
    return jnp.asarray(selq, dtype=BF16), jnp.asarray(selk, dtype=BF16)


def _swa_bias_tables():
    slopes = np.exp2(-(8.0 / A_HEADS) * (np.arange(A_HEADS, dtype=np.float32) + 1.0)).astype(np.float32)
    qpos = SUB + np.arange(SUB)
    kpos = np.arange(2 * SUB)
    dist = kpos[:, None] * 0 + qpos[None, :] - kpos[:, None]
    band = (dist >= 0) & (dist < WINDOW)
    out = np.zeros((A_KV_HEADS, 2, 2 * SUB, A_GROUP * SUB), np.float32)
    for g in range(A_KV_HEADS):
        for j in range(A_GROUP):
            alibi = (-slopes[g * A_GROUP + j] * dist.astype(np.float32)).astype(np.float32) * np.float32(LOG2E)
            cols = slice(j * SUB, (j + 1) * SUB)
            out[g, 0, :, cols] = np.where(band, alibi, NEG_BIG)
            out[g, 1, :, cols] = np.where(band & (kpos >= SUB)[:, None], alibi, NEG_BIG)
    return jnp.asarray(out)


def kernel(x, attn_norm_g, w_in, b_forget, q_norm_a, k_norm_a, sink_logits, q_norm_b, k_norm_b, w_out,
           mlp_norm_g, w_up, w_down):
    b, s, d = x.shape
    assert d == D_MODEL and s % Q_TILE == 0 and s % TOK_TILE == 0
    scale = 1.0 / math.sqrt(HEAD_DIM)

    w_pad = jnp.pad(w_in, ((0, 0), (0, PROJ_W - w_in.shape[1]))).astype(BF16)
    bf_pad = jnp.pad(b_forget.astype(F32), (0, LANES - B_HEADS)).reshape(1, LANES)
    tri = jnp.asarray(np.tril(np.ones((TOK_TILE, TOK_TILE), np.float32)), dtype=BF16)
    e2 = _block_diag_ones()
    selq, selk = _fox_select_tables()
    bias = _swa_bias_tables()
    sink_rows = jnp.repeat(sink_logits.astype(F32).reshape(A_KV_HEADS, 1, A_GROUP) * LOG2E, SUB, axis=2)
    gqa2 = (jnp.tile(q_norm_a.astype(F32), 2) * (scale * LOG2E)).reshape(1, LANES)
    gka2 = jnp.tile(k_norm_a.astype(F32), 2).reshape(1, LANES)
    gqb2 = (jnp.tile(q_norm_b.astype(F32), 2) * (scale * LOG2E)).reshape(1, LANES)
    gkb2 = jnp.tile(k_norm_b.astype(F32), 2).reshape(1, LANES)

    qkv, cp = _in_proj(x, attn_norm_g.astype(F32).reshape(1, d), w_pad, bf_pad, tri)
    out_a = _swa(qkv, gqa2, gka2, e2, bias, sink_rows)
    out_b = _fox(qkv, cp, selq, selk, gqb2, gkb2, e2)

    wo = w_out.astype(BF16)
    y = _out_mlp(x.reshape(b * s, d), out_a.reshape(b * s, -1), out_b.reshape(b * s, -1),
                 wo[:A_HEADS * HEAD_DIM], wo[A_HEADS * HEAD_DIM:],
                 mlp_norm_g.astype(F32).reshape(1, d), w_up.astype(BF16), w_down.astype(BF16))
    return y.reshape(b, s, d)
```

```python
import functools
import math

import numpy as np
import jax
import jax.numpy as jnp
from jax import lax
from jax.experimental import pallas as pl
from jax.experimental.pallas import tpu as pltpu

F32 = jnp.float32
BF16 = jnp.bfloat16

D_MODEL = 1024
HEAD_DIM = 64
A_HEADS = 8
A_KV_HEADS = 2
A_GROUP = A_HEADS // A_KV_HEADS
B_HEADS = 8
WINDOW = 128
D_FF = 4 * D_MODEL
EPS = 1e-6
LOG2E = 1.4426950408889634
NEG_BIG = -1e30

LANES = 128
HALF = LANES // 2

QA_BLK = 0
KA_BLK = 4
VA_BLK = 5
QB_BLK = 6
KB_BLK = 10
VB_BLK = 14
QKV_W = 18 * LANES
PROJ_W = QKV_W + LANES

PIECE_STRIDE = 8
ONE_LANE = 3 * PIECE_STRIDE

TOK_TILE = 512
Q_TILE = 512
PRO_TILE = 512
FOX_STEPS_PER_TRIP = 4
SUB = 128


def _lane_iota(shape):
    return lax.broadcasted_iota(jnp.int32, shape, len(shape) - 1)


def _split3(v):
    hi = v.astype(BF16).astype(F32)
    r = v - hi
    mid = r.astype(BF16).astype(F32)
    lo = (r - mid).astype(BF16).astype(F32)
    return hi, mid, lo


def _head_rms_scale(t, e2):
    sq = t * t
    hi = sq.astype(BF16)
    lo = (sq - hi.astype(F32)).astype(BF16)
    ss = (jnp.dot(hi, e2, preferred_element_type=F32) + jnp.dot(lo, e2, preferred_element_type=F32))
    return lax.rsqrt(ss * (1.0 / HEAD_DIM) + EPS)


def _in_proj_kernel(x_ref, g_ref, w_ref, bf_ref, tri_ref, qkv_ref, cp_ref, carry_ref):
    @pl.when(pl.program_id(1) == 0)
    def _():
        carry_ref[...] = jnp.zeros_like(carry_ref)

    tm = x_ref.shape[1]
    halves = [slice(h * (tm // 2), (h + 1) * (tm // 2)) for h in range(2)]
    xn = []
    for rows in halves:
        x = x_ref[0, rows, :]
        ms = jnp.sum(x * x, axis=-1, keepdims=True) * (1.0 / D_MODEL)
        xn.append((x * lax.rsqrt(ms + EPS) * g_ref[...]).astype(BF16))
    z = jnp.concatenate([jnp.dot(h, w_ref[:, QKV_W:], preferred_element_type=F32) for h in xn], axis=0)
    for rows, h in zip(halves, xn):
        qkv_ref[0, rows, :] = jnp.dot(h, w_ref[:, :QKV_W], preferred_element_type=F32).astype(BF16)

    z = z + bf_ref[...]
    lane = _lane_iota(z.shape)
    logf = jnp.minimum(z, 0.0) - jnp.log(1.0 + jnp.exp(-jnp.abs(z)))
    logf = jnp.where(lane < B_HEADS, logf * LOG2E, 0.0)

    hi, mid, lo = _split3(logf)
    packed = hi + pltpu.roll(mid, PIECE_STRIDE, 1) + pltpu.roll(lo, 2 * PIECE_STRIDE, 1)
    cs = jnp.dot(tri_ref[...], packed.astype(BF16), preferred_element_type=F32)
    cin = cs + pltpu.roll(cs, LANES - PIECE_STRIDE, 1) + pltpu.roll(cs, LANES - 2 * PIECE_STRIDE, 1)
    c = jnp.where(lane < B_HEADS, cin + carry_ref[...], 0.0)
    carry_ref[...] = c[c.shape[0] - 1:, :]

    chi, cmid, clo = _split3(c)
    pieces = (chi + pltpu.roll(cmid, PIECE_STRIDE, 1) + pltpu.roll(clo, 2 * PIECE_STRIDE, 1)
              + jnp.where(lane == ONE_LANE, 1.0, 0.0))
    cp_ref[0] = pieces.astype(BF16)


def _in_proj(x, g, w_pad, bf_pad, tri):
    b, s, d = x.shape
    tm = TOK_TILE
    const = lambda bi, si: (0, 0)
    return pl.pallas_call(
        _in_proj_kernel,
        grid=(b, s // tm),
        in_specs=[
            pl.BlockSpec((1, tm, d), lambda bi, si: (bi, si, 0)),
            pl.BlockSpec((1, d), const),
            pl.BlockSpec((d, PROJ_W), const),
            pl.BlockSpec((1, LANES), const),
            pl.BlockSpec((tm, tm), const),
        ],
        out_specs=[
            pl.BlockSpec((1, tm, QKV_W), lambda bi, si: (bi, si, 0)),
            pl.BlockSpec((1, tm, LANES), lambda bi, si: (bi, si, 0)),
        ],
        out_shape=[
            jax.ShapeDtypeStruct((b, s, QKV_W), BF16),
            jax.ShapeDtypeStruct((b, s, LANES), BF16),
        ],
        scratch_shapes=[pltpu.VMEM((1, LANES), F32)],
        compiler_params=pltpu.CompilerParams(
            dimension_semantics=("arbitrary", "arbitrary"),
            vmem_limit_bytes=48 * 1024 * 1024),
        name="in_proj",
    )(x, g, w_pad, bf_pad, tri)


def _swa_kernel(q_ref, kc_ref, kp_ref, vc_ref, vp_ref, gq_ref, gk_ref, e2_ref, bias_ref, sink_ref, o_ref,
                s_ref, smax_ref, p_ref):
    qi = pl.program_id(1)
    tq = q_ref.shape[1]
    e2 = e2_ref[...]
    variant = jnp.where(qi == 0, 1, 0)
    feat = lax.broadcasted_iota(jnp.int32, (LANES, 1), 0)

    k2 = jnp.concatenate([kp_ref[0], kc_ref[0]], axis=0).astype(F32)
    kn = (k2 * _head_rms_scale(k2, e2) * gk_ref[...]).astype(BF16)
    v_t = jnp.concatenate([vp_ref[0], vc_ref[0]], axis=0).astype(F32).T
    one_row = (HALF, 0)
    v_ts = [jnp.where(feat == one_row[g], 1.0, v_t).astype(BF16) for g in range(A_KV_HEADS)]

    zeros = jnp.zeros((HALF, tq), F32)
    q_ts = []
    for c in range(A_HEADS // 2):
        g = c // (A_GROUP // 2)
        qc = q_ref[0, :, c * LANES:(c + 1) * LANES].astype(F32)
        qn_t = (qc * _head_rms_scale(qc, e2) * gq_ref[...]).T
        for half in range(2):
            f = qn_t[half * HALF:(half + 1) * HALF]
            q_ts.append(jnp.concatenate([f, zeros] if g == 0 else [zeros, f], axis=0).astype(BF16))

    n_sub = tq // SUB
    chains = [(g, r) for g in range(A_KV_HEADS) for r in range(n_sub)]
    for c, (g, r) in enumerate(chains):
        q_t = jnp.concatenate([q_ts[A_GROUP * g + j][:, r * SUB:(r + 1) * SUB] for j in range(A_GROUP)], axis=1)
        s = jnp.dot(kn[r * SUB:r * SUB + 2 * SUB], q_t, preferred_element_type=F32)
        s = s + (bias_ref[g, variant] if r == 0 else bias_ref[g, 0])
        s_ref[c] = s
        smax_ref[c] = jnp.max(s, axis=0, keepdims=True)
    for c, (g, r) in enumerate(chains):
        m = jnp.maximum(smax_ref[c], sink_ref[g])
        p_ref[c] = jnp.exp2(s_ref[c] - m).astype(BF16)
    out_heads = [[] for _ in range(A_HEADS)]
    for c, (g, r) in enumerate(chains):
        sink = sink_ref[g]
        m = jnp.maximum(smax_ref[c], sink)
        pv = jnp.dot(v_ts[g][:, r * SUB:r * SUB + 2 * SUB], p_ref[c], preferred_element_type=F32)
        denom = pv[one_row[g]:one_row[g] + 1, :] + jnp.exp2(sink - m)
        o = pv[g * HALF:(g + 1) * HALF] * (1.0 / denom)
        for j in range(A_GROUP):
            out_heads[A_GROUP * g + j].append(o[:, j * SUB:(j + 1) * SUB])
    for c in range(A_HEADS // 2):
        pair_t = jnp.concatenate([jnp.concatenate(out_heads[2 * c], axis=1),
                                  jnp.concatenate(out_heads[2 * c + 1], axis=1)], axis=0)
        o_ref[0, :, c * LANES:(c + 1) * LANES] = pair_t.T.astype(o_ref.dtype)


def _swa(qkv, gq2, gk2, e2, bias_t, sink_rows):
    b, s, _ = qkv.shape
    tq = Q_TILE
    nsub = tq // SUB
    q_w = A_HEADS * HEAD_DIM
    const = lambda bi, qi: (0, 0)
    return pl.pallas_call(
        _swa_kernel,
        grid=(b, s // tq),
        in_specs=[
            pl.BlockSpec((1, tq, q_w), lambda bi, qi: (bi, qi, 0)),
            pl.BlockSpec((1, tq, LANES), lambda bi, qi: (bi, qi, KA_BLK)),
            pl.BlockSpec((1, SUB, LANES), lambda bi, qi: (bi, jnp.maximum(qi * nsub - 1, 0), KA_BLK)),
            pl.BlockSpec((1, tq, LANES), lambda bi, qi: (bi, qi, VA_BLK)),
            pl.BlockSpec((1, SUB, LANES), lambda bi, qi: (bi, jnp.maximum(qi * nsub - 1, 0), VA_BLK)),
            pl.BlockSpec((1, LANES), const),
            pl.BlockSpec((1, LANES), const),
            pl.BlockSpec((LANES, LANES), const),
            pl.BlockSpec(bias_t.shape, lambda bi, qi: (0, 0, 0, 0)),
            pl.BlockSpec(sink_rows.shape, lambda bi, qi: (0, 0, 0)),
        ],
        out_specs=pl.BlockSpec((1, tq, q_w), lambda bi, qi: (bi, qi, 0)),
        out_shape=jax.ShapeDtypeStruct((b, s, q_w), BF16),
        scratch_shapes=[
            pltpu.VMEM((A_KV_HEADS * nsub, 2 * SUB, A_GROUP * SUB), F32),
            pltpu.VMEM((A_KV_HEADS * nsub, 1, A_GROUP * SUB), F32),
            pltpu.VMEM((A_KV_HEADS * nsub, 2 * SUB, A_GROUP * SUB), BF16),
        ],
        compiler_params=pltpu.CompilerParams(
            dimension_semantics=("arbitrary", "arbitrary"),
            vmem_limit_bytes=48 * 1024 * 1024),
        name="swa",
    )(qkv, qkv, qkv, qkv, qkv, gq2, gk2, e2, bias_t, sink_rows)


def _fox_kernel(q_ref, k_ref, v_ref, cp_ref, selq_ref, selk_ref, gq_ref, gk_ref, e2_ref,
                o_ref, kaug_ref, qt_ref, vt_ref, mask_ref, s_ref, smax_ref, m_ref, acc_ref):
    s_len = k_ref.shape[1]
    tq = tk = PRO_TILE
    n_tiles = s_len // tq
    n_steps = n_tiles * (n_tiles + 1) // 2
    e2 = e2_ref[...]
    lane = _lane_iota((1, LANES))
    low = lane < HALF
    own = (low, jnp.logical_not(low))
    one_lane = (HALF, 0)

    def prepare(i, _):
        rows = pl.ds(pl.multiple_of(i * PRO_TILE, PRO_TILE), PRO_TILE)
        kc = k_ref[0, rows, :].astype(F32)
        kn = kc * _head_rms_scale(kc, e2) * gk_ref[...]
        qc = q_ref[0, rows, :].astype(F32)
        qn = qc * _head_rms_scale(qc, e2) * gq_ref[...]
        vc = v_ref[0, rows, :].astype(F32)
        cp = cp_ref[0, rows, :]
        for hh in range(2):
            extra_k = jnp.dot(cp, selk_ref[0, hh], preferred_element_type=F32)
            kaug_ref[hh, rows, :] = jnp.where(own[hh], kn, extra_k).astype(BF16)
            extra_q = jnp.dot(cp, selq_ref[0, hh], preferred_element_type=F32)
            qt_ref[hh, i] = jnp.where(own[hh], qn, extra_q).astype(BF16).T
            va = jnp.where(own[hh], vc, jnp.where(lane == one_lane[hh], 1.0, 0.0))
            vt_ref[hh, i] = va.astype(BF16).T
        return 0
    lax.fori_loop(0, n_tiles, prepare, 0, unroll=2)

    key = lax.broadcasted_iota(jnp.int32, (tk, tq), 0)
    qry = lax.broadcasted_iota(jnp.int32, (tk, tq), 1)
    mask_ref[0] = jnp.zeros((tk, tq), F32)
    mask_ref[1] = jnp.where(key <= qry, 0.0, NEG_BIG)
    s_ref[...] = jnp.zeros_like(s_ref)
    smax_ref[...] = jnp.zeros_like(smax_ref)
    m_ref[...] = jnp.zeros_like(m_ref)
    acc_ref[...] = jnp.ones_like(acc_ref)

    def issue_scores(hh, qi0, kb0):
        diag = jnp.where(kb0 == qi0, 1, 0)
        krows = pl.ds(pl.multiple_of(kb0 * tk, tk), tk)
        nxt = jnp.dot(kaug_ref[hh, krows, :], qt_ref[hh, qi0], preferred_element_type=F32) + mask_ref[diag]
        s_ref[hh] = nxt
        smax_ref[hh] = jnp.max(nxt, axis=0, keepdims=True)

    def iteration(carry):
        qi0, kb0, qi1, kb1 = carry
        for hh in range(2):
            s = s_ref[hh]
            smax = smax_ref[hh]
            issue_scores(hh, qi0, kb0)
            m_old = jnp.where(kb1 == 0, NEG_BIG, m_ref[hh])
            m_new = jnp.maximum(m_old, smax)
            p = jnp.exp2(s - m_new).astype(BF16)
            acc_ref[hh, qi1] = (jnp.exp2(m_old - m_new) * acc_ref[hh, qi1]
                                + jnp.dot(vt_ref[hh, kb1], p, preferred_element_type=F32))
            m_ref[hh] = m_new
        wrap = kb0 == qi0
        last = jnp.logical_and(wrap, qi0 == n_tiles - 1)
        qi_n = jnp.where(jnp.logical_and(wrap, jnp.logical_not(last)), qi0 + 1, qi0)
        kb_n = jnp.where(last, kb0, jnp.where(wrap, 0, kb0 + 1))
        return qi_n, kb_n, qi0, kb0

    zero = jnp.int32(0)
    for hh in range(2):
        issue_scores(hh, zero, zero)
    first_next = (jnp.int32(1), zero) if n_tiles > 1 else (zero, zero)
    def trip(_, carry):
        for _ in range(FOX_STEPS_PER_TRIP):
            carry = iteration(carry)
        return carry
    carry = lax.fori_loop(0, n_steps // FOX_STEPS_PER_TRIP, trip, (*first_next, zero, zero))
    for _ in range(n_steps % FOX_STEPS_PER_TRIP):
        carry = iteration(carry)

    feat = lax.broadcasted_iota(jnp.int32, (LANES, 1), 0)

    def finish(i, _):
        outs = []
        for hh in range(2):
            acc = acc_ref[hh, i]
            outs.append(acc * (1.0 / acc[one_lane[hh]:one_lane[hh] + 1, :]))
        rows = pl.ds(pl.multiple_of(i * tq, tq), tq)
        o_ref[0, rows, :] = jnp.where(feat < HALF, outs[0], outs[1]).T.astype(o_ref.dtype)
        return 0
    lax.fori_loop(0, n_tiles, finish, 0, unroll=2)


def _fox(qkv, cp, selq, selk, gq2, gk2, e2):
    b, s, _ = qkv.shape
    assert s % PRO_TILE == 0
    npair = B_HEADS // 2
    nblk = s // PRO_TILE
    const = lambda bi, p: (0, 0)
    return pl.pallas_call(
        _fox_kernel,
        grid=(b, npair),
        in_specs=[
            pl.BlockSpec((1, s, LANES), lambda bi, p: (bi, 0, QB_BLK + p)),
            pl.BlockSpec((1, s, LANES), lambda bi, p: (bi, 0, KB_BLK + p)),
            pl.BlockSpec((1, s, LANES), lambda bi, p: (bi, 0, VB_BLK + p)),
            pl.BlockSpec((1, s, LANES), lambda bi, p: (bi, 0, 0)),
            pl.BlockSpec((1, 2, LANES, LANES), lambda bi, p: (p, 0, 0, 0)),
            pl.BlockSpec((1, 2, LANES, LANES), lambda bi, p: (p, 0, 0, 0)),
            pl.BlockSpec((1, LANES), const),
            pl.BlockSpec((1, LANES), const),
            pl.BlockSpec((LANES, LANES), const),
        ],
        out_specs=pl.BlockSpec((1, s, LANES), lambda bi, p: (bi, 0, p)),
        out_shape=jax.ShapeDtypeStruct((b, s, B_HEADS * HEAD_DIM), BF16),
        scratch_shapes=[
            pltpu.VMEM((2, s, LANES), BF16),
            pltpu.VMEM((2, nblk, LANES, PRO_TILE), BF16),
            pltpu.VMEM((2, nblk, LANES, PRO_TILE), BF16),
            pltpu.VMEM((2, PRO_TILE, PRO_TILE), F32),
            pltpu.VMEM((2, PRO_TILE, PRO_TILE), F32),
            pltpu.VMEM((2, 1, PRO_TILE), F32),
            pltpu.VMEM((2, 1, PRO_TILE), F32),
            pltpu.VMEM((2, nblk, LANES, PRO_TILE), F32),
        ],
        compiler_params=pltpu.CompilerParams(
            dimension_semantics=("arbitrary", "arbitrary"),
            vmem_limit_bytes=48 * 1024 * 1024),
        name="fox",
    )(qkv, qkv, qkv, cp, selq, selk, gq2, gk2, e2)


def _out_mlp_kernel(x_ref, oa_ref, ob_ref, woa_ref, wob_ref, g_ref, wup_ref, wdn_ref, y_ref):
    h = (x_ref[...]
         + jnp.dot(oa_ref[...], woa_ref[...], preferred_element_type=F32)
         + jnp.dot(ob_ref[...], wob_ref[...], preferred_element_type=F32))
    ms = jnp.sum(h * h, axis=-1, keepdims=True) * (1.0 / D_MODEL)
    hn = (h * lax.rsqrt(ms + EPS) * g_ref[...]).astype(BF16)
    u = jnp.maximum(jnp.dot(hn, wup_ref[...], preferred_element_type=F32), 0.0)
    y_ref[...] = h + jnp.dot((u * u).astype(BF16), wdn_ref[...], preferred_element_type=F32)


def _out_mlp(x2, oa2, ob2, woa, wob, g2, wup, wdn):
    n, d = x2.shape
    tm = TOK_TILE
    const = lambda i: (0, 0)
    resident = dict(pipeline_mode=pl.Buffered(1))
    return pl.pallas_call(
        _out_mlp_kernel,
        grid=(n // tm,),
        in_specs=[
            pl.BlockSpec((tm, d), lambda i: (i, 0)),
            pl.BlockSpec((tm, oa2.shape[1]), lambda i: (i, 0)),
            pl.BlockSpec((tm, ob2.shape[1]), lambda i: (i, 0)),
            pl.BlockSpec(woa.shape, const, **resident),
            pl.BlockSpec(wob.shape, const, **resident),
            pl.BlockSpec((1, d), const),
            pl.BlockSpec(wup.shape, const, **resident),
            pl.BlockSpec(wdn.shape, const, **resident),
        ],
        out_specs=pl.BlockSpec((tm, d), lambda i: (i, 0)),
        out_shape=jax.ShapeDtypeStruct((n, d), F32),
        compiler_params=pltpu.CompilerParams(
            dimension_semantics=("arbitrary",),
            vmem_limit_bytes=52 * 1024 * 1024),
        name="out_mlp",
    )(x2, oa2, ob2, woa, wob, g2, wup, wdn)


def _block_diag_ones():
    i = np.arange(LANES)
    return jnp.asarray((i[:, None] // HALF == i[None, :] // HALF).astype(np.float32), dtype=BF16)


def _fox_select_tables():
    selq = np.zeros((B_HEADS // 2, 2, LANES, LANES), np.float32)
    selk = np.zeros_like(selq)
    for p in range(B_HEADS // 2):
        for hh in range(2):
            h = 2 * p + hh
            base = HALF if hh == 0 else 0
            for piece in range(3):
                selq[p, hh, piece * PIECE_STRIDE + h, base + piece] = 1.0
                selq[p, hh, ONE_LANE, base + 3 + piece] = 1.0
                selk[p, hh, ONE_LANE, base + piece] = 1.0
                selk[p, hh, piece * PIECE_STRIDE + h, base + 3 + piece] = -1.0
    return jnp.asarray(selq, dtype=BF16), jnp.asarray(selk, dtype=BF16)


def _swa_bias_tables():
    slopes = np.exp2(-(8.0 / A_HEADS) * (np.arange(A_HEADS, dtype=np.float32) + 1.0)).astype(np.float32)
    qpos = SUB + np.arange(SUB)
    kpos = np.arange(2 * SUB)
    dist = kpos[:, None] * 0 + qpos[None, :] - kpos[:, None]
    band = (dist >= 0) & (dist < WINDOW)
    out = np.zeros((A_KV_HEADS, 2, 2 * SUB, A_GROUP * SUB), np.float32)
    for g in range(A_KV_HEADS):
        for j in range(A_GROUP):
            alibi = (-slopes[g * A_GROUP + j] * dist.astype(np.float32)).astype(np.float32) * np.float32(LOG2E)
            cols = slice(j * SUB, (j + 1) * SUB)
            out[g, 0, :, cols] = np.where(band, alibi, NEG_BIG)
            out[g, 1, :, cols] = np.where(band & (kpos >= SUB)[:, None], alibi, NEG_BIG)
    return jnp.asarray(out)


def kernel(x, attn_norm_g, w_in, b_forget, q_norm_a, k_norm_a, sink_logits, q_norm_b, k_norm_b, w_out,
           mlp_norm_g, w_up, w_down):
    b, s, d = x.shape
    assert d == D_MODEL and s % Q_TILE == 0 and s % TOK_TILE == 0
    scale = 1.0 / math.sqrt(HEAD_DIM)

    w_pad = jnp.pad(w_in, ((0, 0), (0, PROJ_W - w_in.shape[1]))).astype(BF16)
    bf_pad = jnp.pad(b_forget.astype(F32), (0, LANES - B_HEADS)).reshape(1, LANES)
    tri = jnp.asarray(np.tril(np.ones((TOK_TILE, TOK_TILE), np.float32)), dtype=BF16)
    e2 = _block_diag_ones()
    selq, selk = _fox_select_tables()
    bias = _swa_bias_tables()
    sink_rows = jnp.repeat(sink_logits.astype(F32).reshape(A_KV_HEADS, 1, A_GROUP) * LOG2E, SUB, axis=2)
    gqa2 = (jnp.tile(q_norm_a.astype(F32), 2) * (scale * LOG2E)).reshape(1, LANES)
    gka2 = jnp.tile(k_norm_a.astype(F32), 2).reshape(1, LANES)
    gqb2 = (jnp.tile(q_norm_b.astype(F32), 2) * (scale * LOG2E)).reshape(1, LANES)
    gkb2 = jnp.tile(k_norm_b.astype(F32), 2).reshape(1, LANES)

    qkv, cp = _in_proj(x, attn_norm_g.astype(F32).reshape(1, d), w_pad, bf_pad, tri)
    out_a = _swa(qkv, gqa2, gka2, e2, bias, sink_rows)
    out_b = _fox(qkv, cp, selq, selk, gqb2, gkb2, e2)

    wo = w_out.astype(BF16)
    y = _out_mlp(x.reshape(b * s, d), out_a.reshape(b * s, -1), out_b.reshape(b * s, -1),
                 wo[:A_HEADS * HEAD_DIM], wo[A_HEADS * HEAD_DIM:],
                 mlp_norm_g.astype(F32).reshape(1, d), w_up.astype(BF16), w_down.astype(BF16))
    return y.reshape(b, s, d)
```

```python
import functools
import math

import numpy as np
import jax
import jax.numpy as jnp
from jax import lax
from jax.experimental import pallas as pl
from jax.experimental.pallas import tpu as pltpu

F32 = jnp.float32
BF16 = jnp.bfloat16

D_MODEL = 1024
HEAD_DIM = 64
A_HEADS = 8
A_KV_HEADS = 2
A_GROUP = A_HEADS // A_KV_HEADS
B_HEADS = 8
WINDOW = 128
D_FF = 4 * D_MODEL
EPS = 1e-6
LOG2E = 1.4426950408889634
NEG_BIG = -1e30

LANES = 128
HALF = LANES // 2

QA_BLK = 0
KA_BLK = 4
VA_BLK = 5
QB_BLK = 6
KB_BLK = 10
VB_BLK = 14
QKV_W = 18 * LANES
PROJ_W = QKV_W + LANES

PIECE_STRIDE = 8
ONE_LANE = 3 * PIECE_STRIDE

TOK_TILE = 512
Q_TILE = 512
PRO_TILE = 512
FOX_STEPS_PER_TRIP = 4
SUB = 128


def _lane_iota(shape):
    return lax.broadcasted_iota(jnp.int32, shape, len(shape) - 1)


def _split3(v):
    hi = v.astype(BF16).astype(F32)
    r = v - hi
    mid = r.astype(BF16).astype(F32)
    lo = (r - mid).astype(BF16).astype(F32)
    return hi, mid, lo


def _head_rms_scale(t, e2):
    sq = t * t
    hi = sq.astype(BF16)
    lo = (sq - hi.astype(F32)).astype(BF16)
    ss = (jnp.dot(hi, e2, preferred_element_type=F32) + jnp.dot(lo, e2, preferred_element_type=F32))
    return lax.rsqrt(ss + HEAD_DIM * EPS)


def _in_proj_kernel(x_ref, g_ref, w_ref, bf_ref, tri_ref, qkv_ref, cp_ref, carry_ref):
    @pl.when(pl.program_id(1) == 0)
    def _():
        carry_ref[...] = jnp.zeros_like(carry_ref)

    tm = x_ref.shape[1]
    halves = [slice(h * (tm // 2), (h + 1) * (tm // 2)) for h in range(2)]
    xn = []
    for rows in halves:
        x = x_ref[0, rows, :]
        ms = jnp.sum(x * x, axis=-1, keepdims=True) * (1.0 / D_MODEL)
        xn.append((x * lax.rsqrt(ms + EPS) * g_ref[...]).astype(BF16))
    z = jnp.concatenate([jnp.dot(h, w_ref[:, QKV_W:], preferred_element_type=F32) for h in xn], axis=0)
    for rows, h in zip(halves, xn):
        qkv_ref[0, rows, :] = jnp.dot(h, w_ref[:, :QKV_W], preferred_element_type=F32).astype(BF16)

    z = z + bf_ref[...]
    lane = _lane_iota(z.shape)
    logf = jnp.minimum(z, 0.0) - jnp.log(1.0 + jnp.exp(-jnp.abs(z)))
    logf = jnp.where(lane < B_HEADS, logf * LOG2E, 0.0)

    hi, mid, lo = _split3(logf)
    packed = hi + pltpu.roll(mid, PIECE_STRIDE, 1) + pltpu.roll(lo, 2 * PIECE_STRIDE, 1)
    cs = jnp.dot(tri_ref[...], packed.astype(BF16), preferred_element_type=F32)
    cin = cs + pltpu.roll(cs, LANES - PIECE_STRIDE, 1) + pltpu.roll(cs, LANES - 2 * PIECE_STRIDE, 1)
    c = jnp.where(lane < B_HEADS, cin + carry_ref[...], 0.0)
    carry_ref[...] = c[c.shape[0] - 1:, :]

    chi, cmid, clo = _split3(c)
    pieces = (chi + pltpu.roll(cmid, PIECE_STRIDE, 1) + pltpu.roll(clo, 2 * PIECE_STRIDE, 1)
              + jnp.where(lane == ONE_LANE, 1.0, 0.0))
    cp_ref[0] = pieces.astype(BF16)


def _in_proj(x, g, w_pad, bf_pad, tri):
    b, s, d = x.shape
    tm = TOK_TILE
    const = lambda bi, si: (0, 0)
    return pl.pallas_call(
        _in_proj_kernel,
        grid=(b, s // tm),
        in_specs=[
            pl.BlockSpec((1, tm, d), lambda bi, si: (bi, si, 0)),
            pl.BlockSpec((1, d), const),
            pl.BlockSpec((d, PROJ_W), const),
            pl.BlockSpec((1, LANES), const),
            pl.BlockSpec((tm, tm), const),
        ],
        out_specs=[
            pl.BlockSpec((1, tm, QKV_W), lambda bi, si: (bi, si, 0)),
            pl.BlockSpec((1, tm, LANES), lambda bi, si: (bi, si, 0)),
        ],
        out_shape=[
            jax.ShapeDtypeStruct((b, s, QKV_W), BF16),
            jax.ShapeDtypeStruct((b, s, LANES), BF16),
        ],
        scratch_shapes=[pltpu.VMEM((1, LANES), F32)],
        compiler_params=pltpu.CompilerParams(
            dimension_semantics=("arbitrary", "arbitrary"),
            vmem_limit_bytes=48 * 1024 * 1024),
        name="in_proj",
    )(x, g, w_pad, bf_pad, tri)


def _swa_kernel(q_ref, kc_ref, kp_ref, vc_ref, vp_ref, gq_ref, gk_ref, e2_ref, bias_ref, sink_ref, o_ref,
                s_ref, smax_ref, p_ref):
    qi = pl.program_id(1)
    tq = q_ref.shape[1]
    e2 = e2_ref[...]
    variant = jnp.where(qi == 0, 1, 0)
    feat = lax.broadcasted_iota(jnp.int32, (LANES, 1), 0)

    k2 = jnp.concatenate([kp_ref[0], kc_ref[0]], axis=0).astype(F32)
    kn = (k2 * _head_rms_scale(k2, e2) * gk_ref[...]).astype(BF16)
    v_t = jnp.concatenate([vp_ref[0], vc_ref[0]], axis=0).astype(F32).T
    one_row = (HALF, 0)
    v_ts = [jnp.where(feat == one_row[g], 1.0, v_t).astype(BF16) for g in range(A_KV_HEADS)]

    zeros = jnp.zeros((HALF, tq), F32)
    q_ts = []
    for c in range(A_HEADS // 2):
        g = c // (A_GROUP // 2)
        qc = q_ref[0, :, c * LANES:(c + 1) * LANES].astype(F32)
        qn_t = (qc * _head_rms_scale(qc, e2) * gq_ref[...]).T
        for half in range(2):
            f = qn_t[half * HALF:(half + 1) * HALF]
            q_ts.append(jnp.concatenate([f, zeros] if g == 0 else [zeros, f], axis=0).astype(BF16))

    n_sub = tq // SUB
    chains = [(g, r) for g in range(A_KV_HEADS) for r in range(n_sub)]
    for c, (g, r) in enumerate(chains):
        q_t = jnp.concatenate([q_ts[A_GROUP * g + j][:, r * SUB:(r + 1) * SUB] for j in range(A_GROUP)], axis=1)
        s = jnp.dot(kn[r * SUB:r * SUB + 2 * SUB], q_t, preferred_element_type=F32)
        s = s + (bias_ref[g, variant] if r == 0 else bias_ref[g, 0])
        s_ref[c] = s
        smax_ref[c] = jnp.max(s, axis=0, keepdims=True)
    for c, (g, r) in enumerate(chains):
        m = jnp.maximum(smax_ref[c], sink_ref[g])
        p_ref[c] = jnp.exp2(s_ref[c] - m).astype(BF16)
    out_heads = [[] for _ in range(A_HEADS)]
    for c, (g, r) in enumerate(chains):
        sink = sink_ref[g]
        m = jnp.maximum(smax_ref[c], sink)
        pv = jnp.dot(v_ts[g][:, r * SUB:r * SUB + 2 * SUB], p_ref[c], preferred_element_type=F32)
        denom = pv[one_row[g]:one_row[g] + 1, :] + jnp.exp2(sink - m)
        o = pv[g * HALF:(g + 1) * HALF] * (1.0 / denom)
        for j in range(A_GROUP):
            out_heads[A_GROUP * g + j].append(o[:, j * SUB:(j + 1) * SUB])
    for c in range(A_HEADS // 2):
        pair_t = jnp.concatenate([jnp.concatenate(out_heads[2 * c], axis=1),
                                  jnp.concatenate(out_heads[2 * c + 1], axis=1)], axis=0)
        o_ref[0, :, c * LANES:(c + 1) * LANES] = pair_t.T.astype(o_ref.dtype)


def _swa(qkv, gq2, gk2, e2, bias_t, sink_rows):
    b, s, _ = qkv.shape
    tq = Q_TILE
    nsub = tq // SUB
    q_w = A_HEADS * HEAD_DIM
    const = lambda bi, qi: (0, 0)
    return pl.pallas_call(
        _swa_kernel,
        grid=(b, s // tq),
        in_specs=[
            pl.BlockSpec((1, tq, q_w), lambda bi, qi: (bi, qi, 0)),
            pl.BlockSpec((1, tq, LANES), lambda bi, qi: (bi, qi, KA_BLK)),
            pl.BlockSpec((1, SUB, LANES), lambda bi, qi: (bi, jnp.maximum(qi * nsub - 1, 0), KA_BLK)),
            pl.BlockSpec((1, tq, LANES), lambda bi, qi: (bi, qi, VA_BLK)),
            pl.BlockSpec((1, SUB, LANES), lambda bi, qi: (bi, jnp.maximum(qi * nsub - 1, 0), VA_BLK)),
            pl.BlockSpec((1, LANES), const),
            pl.BlockSpec((1, LANES), const),
            pl.BlockSpec((LANES, LANES), const),
            pl.BlockSpec(bias_t.shape, lambda bi, qi: (0, 0, 0, 0)),
            pl.BlockSpec(sink_rows.shape, lambda bi, qi: (0, 0, 0)),
        ],
        out_specs=pl.BlockSpec((1, tq, q_w), lambda bi, qi: (bi, qi, 0)),
        out_shape=jax.ShapeDtypeStruct((b, s, q_w), BF16),
        scratch_shapes=[
            pltpu.VMEM((A_KV_HEADS * nsub, 2 * SUB, A_GROUP * SUB), F32),
            pltpu.VMEM((A_KV_HEADS * nsub, 1, A_GROUP * SUB), F32),
            pltpu.VMEM((A_KV_HEADS * nsub, 2 * SUB, A_GROUP * SUB), BF16),
        ],
        compiler_params=pltpu.CompilerParams(
            dimension_semantics=("arbitrary", "arbitrary"),
            vmem_limit_bytes=48 * 1024 * 1024),
        name="swa",
    )(qkv, qkv, qkv, qkv, qkv, gq2, gk2, e2, bias_t, sink_rows)


def _fox_kernel(q_ref, k_ref, v_ref, cp_ref, selq_ref, selk_ref, gq_ref, gk_ref, e2_ref,
                o_ref, kaug_ref, qt_ref, vt_ref, mask_ref, s_ref, smax_ref, m_ref, acc_ref):
    s_len = k_ref.shape[1]
    tq = tk = PRO_TILE
    n_tiles = s_len // tq
    n_steps = n_tiles * (n_tiles + 1) // 2
    e2 = e2_ref[...]
    lane = _lane_iota((1, LANES))
    low = lane < HALF
    own = (low, jnp.logical_not(low))
    one_lane = (HALF, 0)

    def prepare(i, _):
        rows = pl.ds(pl.multiple_of(i * PRO_TILE, PRO_TILE), PRO_TILE)
        kc = k_ref[0, rows, :].astype(F32)
        kn = kc * _head_rms_scale(kc, e2) * gk_ref[...]
        qc = q_ref[0, rows, :].astype(F32)
        qn = qc * _head_rms_scale(qc, e2) * gq_ref[...]
        vc = v_ref[0, rows, :].astype(F32)
        cp = cp_ref[0, rows, :]
        extra_k = jnp.dot(cp, selk_ref[0], preferred_element_type=F32)
        extra_q = jnp.dot(cp, selq_ref[0], preferred_element_type=F32)
        for hh in range(2):
            kaug_ref[hh, rows, :] = jnp.where(own[hh], kn, extra_k).astype(BF16)
            qt_ref[hh, i] = jnp.where(own[hh], qn, extra_q).astype(BF16).T
            va = jnp.where(own[hh], vc, jnp.where(lane == one_lane[hh], 1.0, 0.0))
            vt_ref[hh, i] = va.astype(BF16).T
        return 0
    lax.fori_loop(0, n_tiles, prepare, 0, unroll=2)

    @pl.when(jnp.logical_and(pl.program_id(0) == 0, pl.program_id(1) == 0))
    def _build_mask():
        key = lax.broadcasted_iota(jnp.int32, (tk, tq), 0)
        qry = lax.broadcasted_iota(jnp.int32, (tk, tq), 1)
        mask_ref[...] = jnp.where(key <= qry, 0.0, NEG_BIG)

    m_ref[...] = jnp.full_like(m_ref, NEG_BIG)
    acc_ref[...] = jnp.zeros_like(acc_ref)

    def issue_scores(hh, qi, kb, masked):
        krows = pl.ds(pl.multiple_of(kb * tk, tk), tk)
        nxt = jnp.dot(kaug_ref[hh, krows, :], qt_ref[hh, qi], preferred_element_type=F32)
        if masked:
            nxt = nxt + mask_ref[...]
        s_ref[hh] = nxt
        smax_ref[hh] = jnp.max(nxt, axis=0, keepdims=True)

    half = tk // 2

    def issue_diagonal(hh, d):
        r0 = d * tk
        q_t = qt_ref[hh, d]
        top = jnp.dot(kaug_ref[hh, r0:r0 + half, :], q_t, preferred_element_type=F32) + mask_ref[:half, :]
        bot = (jnp.dot(kaug_ref[hh, r0 + half:r0 + tk, :], q_t[:, half:], preferred_element_type=F32)
               + mask_ref[half:, half:])
        s_ref[hh, :half, :] = top
        s_ref[hh, half:, half:] = bot
        s_ref[hh, half:, :half] = jnp.full((half, half), NEG_BIG, F32)
        top_max = jnp.max(top, axis=0, keepdims=True)
        smax_ref[hh] = jnp.concatenate(
            [top_max[:, :half], jnp.maximum(top_max[:, half:], jnp.max(bot, axis=0, keepdims=True))], axis=1)

    def consume(hh, qi1, kb1, s, smax):
        m_old = m_ref[hh, qi1]
        m_new = jnp.maximum(m_old, smax)
        p = jnp.exp2(s - m_new).astype(BF16)
        acc_ref[hh, qi1] = (jnp.exp2(m_old - m_new) * acc_ref[hh, qi1]
                            + jnp.dot(vt_ref[hh, kb1], p, preferred_element_type=F32))
        m_ref[hh, qi1] = m_new

    def consume_diagonal(hh, d, s_top, s_bot, smax):
        m_old = m_ref[hh, d]
        m_new = jnp.maximum(m_old, smax)
        p_top = jnp.exp2(s_top - m_new).astype(BF16)
        p_bot = jnp.exp2(s_bot - m_new[:, half:]).astype(BF16)
        v_t = vt_ref[hh, d]
        pv = jnp.dot(v_t[:, :half], p_top, preferred_element_type=F32)
        pv_late = jnp.dot(v_t[:, half:], p_bot, preferred_element_type=F32)
        acc = jnp.exp2(m_old - m_new) * acc_ref[hh, d] + pv
        acc_ref[hh, d] = jnp.concatenate([acc[:, :half], acc[:, half:] + pv_late], axis=1)
        m_ref[hh, d] = m_new

    n_off = n_tiles * (n_tiles - 1) // 2
    zero = jnp.int32(0)
    for hh in range(2):
        issue_diagonal(hh, 0)

    def off_diagonal(carry):
        qi0, kb0, qi1, kb1 = carry
        for hh in range(2):
            s_prev, smax_prev = s_ref[hh], smax_ref[hh]
            issue_scores(hh, qi0, kb0, False)
            consume(hh, qi1, kb1, s_prev, smax_prev)
        wrap = kb0 + 1 == qi0
        return jnp.where(wrap, qi0 + 1, qi0), jnp.where(wrap, 0, kb0 + 1), qi0, kb0

    def trip(_, carry):
        for _ in range(FOX_STEPS_PER_TRIP):
            carry = off_diagonal(carry)
        return carry
    carry = (jnp.int32(1), zero, zero, zero)
    carry = lax.fori_loop(0, n_off // FOX_STEPS_PER_TRIP, trip, carry)
    for _ in range(n_off % FOX_STEPS_PER_TRIP):
        carry = off_diagonal(carry)
    for d in range(1, n_tiles):
        for hh in range(2):
            if d == 1:
                s_prev, smax_prev = s_ref[hh], smax_ref[hh]
                issue_diagonal(hh, d)
                consume(hh, carry[2], carry[3], s_prev, smax_prev)
            else:
                s_top, s_bot, smax_prev = s_ref[hh, :half, :], s_ref[hh, half:, half:], smax_ref[hh]
                issue_diagonal(hh, d)
                consume_diagonal(hh, d - 1, s_top, s_bot, smax_prev)
    for hh in range(2):
        consume_diagonal(hh, n_tiles - 1, s_ref[hh, :half, :], s_ref[hh, half:, half:], smax_ref[hh])

    feat = lax.broadcasted_iota(jnp.int32, (LANES, 1), 0)

    def finish(i, _):
        outs = []
        for hh in range(2):
            acc = acc_ref[hh, i]
            outs.append(acc * (1.0 / acc[one_lane[hh]:one_lane[hh] + 1, :]))
        rows = pl.ds(pl.multiple_of(i * tq, tq), tq)
        o_ref[0, rows, :] = jnp.where(feat < HALF, outs[0], outs[1]).T.astype(o_ref.dtype)
        return 0
    lax.fori_loop(0, n_tiles, finish, 0, unroll=2)


def _fox(qkv, cp, selq, selk, gq2, gk2, e2):
    b, s, _ = qkv.shape
    assert s % PRO_TILE == 0
    npair = B_HEADS // 2
    nblk = s // PRO_TILE
    const = lambda bi, p: (0, 0)
    return pl.pallas_call(
        _fox_kernel,
        grid=(b, npair),
        in_specs=[
            pl.BlockSpec((1, s, LANES), lambda bi, p: (bi, 0, QB_BLK + p)),
            pl.BlockSpec((1, s, LANES), lambda bi, p: (bi, 0, KB_BLK + p)),
            pl.BlockSpec((1, s, LANES), lambda bi, p: (bi, 0, VB_BLK + p)),
            pl.BlockSpec((1, s, LANES), lambda bi, p: (bi, 0, 0)),
            pl.BlockSpec((1, LANES, LANES), lambda bi, p: (p, 0, 0)),
            pl.BlockSpec((1, LANES, LANES), lambda bi, p: (p, 0, 0)),
            pl.BlockSpec((1, LANES), const),
            pl.BlockSpec((1, LANES), const),
            pl.BlockSpec((LANES, LANES), const),
        ],
        out_specs=pl.BlockSpec((1, s, LANES), lambda bi, p: (bi, 0, p)),
        out_shape=jax.ShapeDtypeStruct((b, s, B_HEADS * HEAD_DIM), BF16),
        scratch_shapes=[
            pltpu.VMEM((2, s, LANES), BF16),
            pltpu.VMEM((2, nblk, LANES, PRO_TILE), BF16),
            pltpu.VMEM((2, nblk, LANES, PRO_TILE), BF16),
            pltpu.VMEM((PRO_TILE, PRO_TILE), F32),
            pltpu.VMEM((2, PRO_TILE, PRO_TILE), F32),
            pltpu.VMEM((2, 1, PRO_TILE), F32),
            pltpu.VMEM((2, nblk, 1, PRO_TILE), F32),
            pltpu.VMEM((2, nblk, LANES, PRO_TILE), F32),
        ],
        compiler_params=pltpu.CompilerParams(
            dimension_semantics=("arbitrary", "arbitrary"),
            vmem_limit_bytes=48 * 1024 * 1024),
        name="fox",
    )(qkv, qkv, qkv, cp, selq, selk, gq2, gk2, e2)


def _out_mlp_kernel(x_ref, oa_ref, ob_ref, woa_ref, wob_ref, g_ref, wup_ref, wdn_ref, y_ref):
    h = (x_ref[...]
         + jnp.dot(oa_ref[...], woa_ref[...], preferred_element_type=F32)
         + jnp.dot(ob_ref[...], wob_ref[...], preferred_element_type=F32))
    ms = jnp.sum(h * h, axis=-1, keepdims=True) * (1.0 / D_MODEL)
    hn = (h * lax.rsqrt(ms + EPS) * g_ref[...]).astype(BF16)
    u = jnp.maximum(jnp.dot(hn, wup_ref[...], preferred_element_type=F32), 0.0)
    y_ref[...] = h + jnp.dot((u * u).astype(BF16), wdn_ref[...], preferred_element_type=F32)


def _out_mlp(x2, oa2, ob2, woa, wob, g2, wup, wdn):
    n, d = x2.shape
    tm = TOK_TILE
    const = lambda i: (0, 0)
    resident = dict(pipeline_mode=pl.Buffered(1))
    return pl.pallas_call(
        _out_mlp_kernel,
        grid=(n // tm,),
        in_specs=[
            pl.BlockSpec((tm, d), lambda i: (i, 0)),
            pl.BlockSpec((tm, oa2.shape[1]), lambda i: (i, 0)),
            pl.BlockSpec((tm, ob2.shape[1]), lambda i: (i, 0)),
            pl.BlockSpec(woa.shape, const, **resident),
            pl.BlockSpec(wob.shape, const, **resident),
            pl.BlockSpec((1, d), const),
            pl.BlockSpec(wup.shape, const, **resident),
            pl.BlockSpec(wdn.shape, const, **resident),
        ],
        out_specs=pl.BlockSpec((tm, d), lambda i: (i, 0)),
        out_shape=jax.ShapeDtypeStruct((n, d), F32),
        compiler_params=pltpu.CompilerParams(
            dimension_semantics=("arbitrary",),
            vmem_limit_bytes=52 * 1024 * 1024),
        name="out_mlp",
    )(x2, oa2, ob2, woa, wob, g2, wup, wdn)


def _block_diag_ones():
    i = np.arange(LANES)
    return jnp.asarray((i[:, None] // HALF == i[None, :] // HALF).astype(np.float32), dtype=BF16)


def _fox_select_tables():
    selq = np.zeros((B_HEADS // 2, LANES, LANES), np.float32)
    selk = np.zeros_like(selq)
    for p in range(B_HEADS // 2):
        for hh in range(2):
            h = 2 * p + hh
            base = HALF if hh == 0 else 0
            for piece in range(3):
                selq[p, piece * PIECE_STRIDE + h, base + piece] = 1.0
                selq[p, ONE_LANE, base + 3 + piece] = 1.0
                selk[p, ONE_LANE, base + piece] = 1.0
                selk[p, piece * PIECE_STRIDE + h, base + 3 + piece] = -1.0
    return jnp.asarray(selq, dtype=BF16), jnp.asarray(selk, dtype=BF16)


def _swa_bias_tables():
    slopes = np.exp2(-(8.0 / A_HEADS) * (np.arange(A_HEADS, dtype=np.float32) + 1.0)).astype(np.float32)
    qpos = SUB + np.arange(SUB)
    kpos = np.arange(2 * SUB)
    dist = kpos[:, None] * 0 + qpos[None, :] - kpos[:, None]
    band = (dist >= 0) & (dist < WINDOW)
    out = np.zeros((A_KV_HEADS, 2, 2 * SUB, A_GROUP * SUB), np.float32)
    for g in range(A_KV_HEADS):
        for j in range(A_GROUP):
            alibi = (-slopes[g * A_GROUP + j] * dist.astype(np.float32)).astype(np.float32) * np.float32(LOG2E)
            cols = slice(j * SUB, (j + 1) * SUB)
            out[g, 0, :, cols] = np.where(band, alibi, NEG_BIG)
            out[g, 1, :, cols] = np.where(band & (kpos >= SUB)[:, None], alibi, NEG_BIG)
    return jnp.asarray(out)


def kernel(x, attn_norm_g, w_in, b_forget, q_norm_a, k_norm_a, sink_logits, q_norm_b, k_norm_b, w_out,
           mlp_norm_g, w_up, w_down):
    b, s, d = x.shape
    assert d == D_MODEL and s % Q_TILE == 0 and s % TOK_TILE == 0
    scale = 1.0 / math.sqrt(HEAD_DIM)

    w_pad = jnp.pad(w_in, ((0, 0), (0, PROJ_W - w_in.shape[1]))).astype(BF16)
    bf_pad = jnp.pad(b_forget.astype(F32), (0, LANES - B_HEADS)).reshape(1, LANES)
    tri = jnp.asarray(np.tril(np.ones((TOK_TILE, TOK_TILE), np.float32)), dtype=BF16)
    e2 = _block_diag_ones()
    selq, selk = _fox_select_tables()
    bias = _swa_bias_tables()
    sink_rows = jnp.repeat(sink_logits.astype(F32).reshape(A_KV_HEADS, 1, A_GROUP) * LOG2E, SUB, axis=2)
    root = math.sqrt(HEAD_DIM)
    gqa2 = (jnp.tile(q_norm_a.astype(F32), 2) * (root * scale * LOG2E)).reshape(1, LANES)
    gka2 = (jnp.tile(k_norm_a.astype(F32), 2) * root).reshape(1, LANES)
    gqb2 = (jnp.tile(q_norm_b.astype(F32), 2) * (root * scale * LOG2E)).reshape(1, LANES)
    gkb2 = (jnp.tile(k_norm_b.astype(F32), 2) * root).reshape(1, LANES)

    qkv, cp = _in_proj(x, attn_norm_g.astype(F32).reshape(1, d), w_pad, bf_pad, tri)
    out_a = _swa(qkv, gqa2, gka2, e2, bias, sink_rows)
    out_b = _fox(qkv, cp, selq, selk, gqb2, gkb2, e2)

    wo = w_out.astype(BF16)
    y = _out_mlp(x.reshape(b * s, d), out_a.reshape(b * s, -1), out_b.reshape(b * s, -1),
                 wo[:A_HEADS * HEAD_DIM], wo[A_HEADS * HEAD_DIM:],
                 mlp_norm_g.astype(F32).reshape(1, d), w_up.astype(BF16), w_down.astype(BF16))
    return y.reshape(b, s, d)
```

```python
import functools
import math

import numpy as np
import jax
import jax.numpy as jnp
from jax import lax
from jax.experimental import pallas as pl
from jax.experimental.pallas import tpu as pltpu

F32 = jnp.float32
BF16 = jnp.bfloat16

D_MODEL = 1024
HEAD_DIM = 64
A_HEADS = 8
A_KV_HEADS = 2
A_GROUP = A_HEADS // A_KV_HEADS
B_HEADS = 8
WINDOW = 128
D_FF = 4 * D_MODEL
EPS = 1e-6
LOG2E = 1.4426950408889634
NEG_BIG = -1e30

LANES = 128
HALF = LANES // 2

QA_BLK = 0
KA_BLK = 4
VA_BLK = 5
QB_BLK = 6
KB_BLK = 10
VB_BLK = 14
QKV_W = 18 * LANES
PROJ_W = QKV_W + LANES

PIECE_STRIDE = 8
ONE_LANE = 3 * PIECE_STRIDE

TOK_TILE = 512
Q_TILE = 512
PRO_TILE = 512
FOX_STEPS_PER_TRIP = 4
SUB = 128


def _lane_iota(shape):
    return lax.broadcasted_iota(jnp.int32, shape, len(shape) - 1)


def _split3(v):
    hi = v.astype(BF16).astype(F32)
    r = v - hi
    mid = r.astype(BF16).astype(F32)
    lo = (r - mid).astype(BF16).astype(F32)
    return hi, mid, lo


def _head_rms_scale(t, e2):
    sq = t * t
    hi = sq.astype(BF16)
    lo = (sq - hi.astype(F32)).astype(BF16)
    ss = (jnp.dot(hi, e2, preferred_element_type=F32) + jnp.dot(lo, e2, preferred_element_type=F32))
    return lax.rsqrt(ss + HEAD_DIM * EPS)


def _in_proj_kernel(x_ref, g_ref, w_ref, bf_ref, qkv_ref, cp_ref, carry_ref):
    @pl.when(pl.program_id(1) == 0)
    def _():
        carry_ref[...] = jnp.zeros_like(carry_ref)

    tm = x_ref.shape[1]
    halves = [slice(h * (tm // 2), (h + 1) * (tm // 2)) for h in range(2)]
    xn = []
    for rows in halves:
        x = x_ref[0, rows, :]
        ms = jnp.sum(x * x, axis=-1, keepdims=True) * (1.0 / D_MODEL)
        xn.append((x * lax.rsqrt(ms + EPS) * g_ref[...]).astype(BF16))
    qkv_ref[0, halves[0], :] = jnp.dot(xn[0], w_ref[:, :QKV_W], preferred_element_type=F32).astype(BF16)
    z = jnp.concatenate([jnp.dot(h, w_ref[:, QKV_W:], preferred_element_type=F32) for h in xn], axis=0)
    qkv_ref[0, halves[1], :] = jnp.dot(xn[1], w_ref[:, :QKV_W], preferred_element_type=F32).astype(BF16)

    z = z + bf_ref[...]
    lane = _lane_iota(z.shape)
    logf = jnp.minimum(z, 0.0) - jnp.log(1.0 + jnp.exp(-jnp.abs(z)))
    logf = jnp.where(lane < B_HEADS, logf * LOG2E, 0.0)

    row = lax.broadcasted_iota(jnp.int32, z.shape, 0)
    c = logf
    shift = 1
    while shift < tm:
        c = c + jnp.where(row >= shift, pltpu.roll(c, shift, 0), 0.0)
        shift *= 2
    c = c + carry_ref[...]
    carry_ref[...] = c[tm - 1:, :]

    chi, cmid, clo = _split3(c)
    pieces = (chi + pltpu.roll(cmid, PIECE_STRIDE, 1) + pltpu.roll(clo, 2 * PIECE_STRIDE, 1)
              + jnp.where(lane == ONE_LANE, 1.0, 0.0))
    cp_ref[0] = pieces.astype(BF16)


def _in_proj(x, g, w_pad, bf_pad):
    b, s, d = x.shape
    tm = TOK_TILE
    const = lambda bi, si: (0, 0)
    return pl.pallas_call(
        _in_proj_kernel,
        grid=(b, s // tm),
        in_specs=[
            pl.BlockSpec((1, tm, d), lambda bi, si: (bi, si, 0)),
            pl.BlockSpec((1, d), const),
            pl.BlockSpec((d, PROJ_W), const),
            pl.BlockSpec((1, LANES), const),
        ],
        out_specs=[
            pl.BlockSpec((1, tm, QKV_W), lambda bi, si: (bi, si, 0)),
            pl.BlockSpec((1, tm, LANES), lambda bi, si: (bi, si, 0)),
        ],
        out_shape=[
            jax.ShapeDtypeStruct((b, s, QKV_W), BF16),
            jax.ShapeDtypeStruct((b, s, LANES), BF16),
        ],
        scratch_shapes=[pltpu.VMEM((1, LANES), F32)],
        compiler_params=pltpu.CompilerParams(
            dimension_semantics=("arbitrary", "arbitrary"),
            vmem_limit_bytes=48 * 1024 * 1024),
        name="in_proj",
    )(x, g, w_pad, bf_pad)


def _swa_kernel(q_ref, kc_ref, kp_ref, vc_ref, vp_ref, gq_ref, gk_ref, e2_ref, bias_ref, sink_ref, o_ref,
                s_ref, smax_ref, p_ref):
    qi = pl.program_id(1)
    tq = q_ref.shape[1]
    e2 = e2_ref[...]
    variant = jnp.where(qi == 0, 1, 0)
    feat = lax.broadcasted_iota(jnp.int32, (LANES, 1), 0)

    k2 = jnp.concatenate([kp_ref[0], kc_ref[0]], axis=0).astype(F32)
    kn = (k2 * _head_rms_scale(k2, e2) * gk_ref[...]).astype(BF16)
    v_t = jnp.concatenate([vp_ref[0], vc_ref[0]], axis=0).T
    one_row = (HALF, 0)
    v_ts = [jnp.where(feat == one_row[g], jnp.ones((), BF16), v_t) for g in range(A_KV_HEADS)]

    zeros = jnp.zeros((HALF, tq), BF16)
    q_ts = []
    for c in range(A_HEADS // 2):
        g = c // (A_GROUP // 2)
        qc = q_ref[0, :, c * LANES:(c + 1) * LANES].astype(F32)
        qn_t = (qc * _head_rms_scale(qc, e2) * gq_ref[...]).astype(BF16).T
        for half in range(2):
            f = qn_t[half * HALF:(half + 1) * HALF]
            q_ts.append(jnp.concatenate([f, zeros] if g == 0 else [zeros, f], axis=0))

    n_sub = tq // SUB
    chains = [(g, r) for g in range(A_KV_HEADS) for r in range(n_sub)]
    for c, (g, r) in enumerate(chains):
        q_t = jnp.concatenate([q_ts[A_GROUP * g + j][:, r * SUB:(r + 1) * SUB] for j in range(A_GROUP)], axis=1)
        s = jnp.dot(kn[r * SUB:r * SUB + 2 * SUB], q_t, preferred_element_type=F32)
        s = s + (bias_ref[g, variant] if r == 0 else bias_ref[g, 0])
        s_ref[c] = s
        smax_ref[c] = jnp.max(s, axis=0, keepdims=True)
    for c, (g, r) in enumerate(chains):
        m = jnp.maximum(smax_ref[c], sink_ref[g])
        p_ref[c] = jnp.exp2(s_ref[c] - m).astype(BF16)
    out_heads = [[] for _ in range(A_HEADS)]
    for c, (g, r) in enumerate(chains):
        sink = sink_ref[g]
        m = jnp.maximum(smax_ref[c], sink)
        pv = jnp.dot(v_ts[g][:, r * SUB:r * SUB + 2 * SUB], p_ref[c], preferred_element_type=F32)
        denom = pv[one_row[g]:one_row[g] + 1, :] + jnp.exp2(sink - m)
        o = pv[g * HALF:(g + 1) * HALF] * (1.0 / denom)
        for j in range(A_GROUP):
            out_heads[A_GROUP * g + j].append(o[:, j * SUB:(j + 1) * SUB])
    for c in range(A_HEADS // 2):
        pair_t = jnp.concatenate([jnp.concatenate(out_heads[2 * c], axis=1),
                                  jnp.concatenate(out_heads[2 * c + 1], axis=1)], axis=0)
        o_ref[0, :, c * LANES:(c + 1) * LANES] = pair_t.astype(o_ref.dtype).T


def _swa(qkv, gq2, gk2, e2, bias_t, sink_rows):
    b, s, _ = qkv.shape
    tq = Q_TILE
    nsub = tq // SUB
    q_w = A_HEADS * HEAD_DIM
    const = lambda bi, qi: (0, 0)
    return pl.pallas_call(
        _swa_kernel,
        grid=(b, s // tq),
        in_specs=[
            pl.BlockSpec((1, tq, q_w), lambda bi, qi: (bi, qi, 0)),
            pl.BlockSpec((1, tq, LANES), lambda bi, qi: (bi, qi, KA_BLK)),
            pl.BlockSpec((1, SUB, LANES), lambda bi, qi: (bi, jnp.maximum(qi * nsub - 1, 0), KA_BLK)),
            pl.BlockSpec((1, tq, LANES), lambda bi, qi: (bi, qi, VA_BLK)),
            pl.BlockSpec((1, SUB, LANES), lambda bi, qi: (bi, jnp.maximum(qi * nsub - 1, 0), VA_BLK)),
            pl.BlockSpec((1, LANES), const),
            pl.BlockSpec((1, LANES), const),
            pl.BlockSpec((LANES, LANES), const),
            pl.BlockSpec(bias_t.shape, lambda bi, qi: (0, 0, 0, 0)),
            pl.BlockSpec(sink_rows.shape, lambda bi, qi: (0, 0, 0)),
        ],
        out_specs=pl.BlockSpec((1, tq, q_w), lambda bi, qi: (bi, qi, 0)),
        out_shape=jax.ShapeDtypeStruct((b, s, q_w), BF16),
        scratch_shapes=[
            pltpu.VMEM((A_KV_HEADS * nsub, 2 * SUB, A_GROUP * SUB), F32),
            pltpu.VMEM((A_KV_HEADS * nsub, 1, A_GROUP * SUB), F32),
            pltpu.VMEM((A_KV_HEADS * nsub, 2 * SUB, A_GROUP * SUB), BF16),
        ],
        compiler_params=pltpu.CompilerParams(
            dimension_semantics=("arbitrary", "arbitrary"),
            vmem_limit_bytes=48 * 1024 * 1024),
        name="swa",
    )(qkv, qkv, qkv, qkv, qkv, gq2, gk2, e2, bias_t, sink_rows)


def _fox_kernel(q_ref, k_ref, v_ref, cp_ref, selq_ref, selk_ref, gq_ref, gk_ref, e2_ref,
                o_ref, kaug_ref, qt_ref, vt_ref, mask_ref, s_ref, smax_ref, m_ref, acc_ref):
    s_len = k_ref.shape[1]
    tq = tk = PRO_TILE
    n_tiles = s_len // tq
    n_steps = n_tiles * (n_tiles + 1) // 2
    e2 = e2_ref[...]
    lane = _lane_iota((1, LANES))
    low = lane < HALF
    own = (low, jnp.logical_not(low))
    one_lane = (HALF, 0)

    def prepare(i, _):
        rows = pl.ds(pl.multiple_of(i * PRO_TILE, PRO_TILE), PRO_TILE)
        kc = k_ref[0, rows, :].astype(F32)
        kn = kc * _head_rms_scale(kc, e2) * gk_ref[...]
        qc = q_ref[0, rows, :].astype(F32)
        qn = qc * _head_rms_scale(qc, e2) * gq_ref[...]
        vc = v_ref[0, rows, :].astype(F32)
        cp = cp_ref[0, rows, :]
        extra_k = jnp.dot(cp, selk_ref[0], preferred_element_type=F32)
        extra_q = jnp.dot(cp, selq_ref[0], preferred_element_type=F32)
        for hh in range(2):
            kaug_ref[hh, rows, :] = jnp.where(own[hh], kn, extra_k).astype(BF16)
            qt_ref[hh, i] = jnp.where(own[hh], qn, extra_q).astype(BF16).T
            va = jnp.where(own[hh], vc, jnp.where(lane == one_lane[hh], 1.0, 0.0))
            vt_ref[hh, i] = va.astype(BF16).T
        return 0
    lax.fori_loop(0, n_tiles, prepare, 0, unroll=2)

    @pl.when(jnp.logical_and(pl.program_id(0) == 0, pl.program_id(1) == 0))
    def _build_mask():
        key = lax.broadcasted_iota(jnp.int32, (tk, tq), 0)
        qry = lax.broadcasted_iota(jnp.int32, (tk, tq), 1)
        mask_ref[...] = jnp.where(key <= qry, 0.0, NEG_BIG)

    m_ref[...] = jnp.full_like(m_ref, NEG_BIG)
    acc_ref[...] = jnp.zeros_like(acc_ref)

    def issue_scores(hh, qi, kb, masked):
        krows = pl.ds(pl.multiple_of(kb * tk, tk), tk)
        nxt = jnp.dot(kaug_ref[hh, krows, :], qt_ref[hh, qi], preferred_element_type=F32)
        if masked:
            nxt = nxt + mask_ref[...]
        s_ref[hh] = nxt
        smax_ref[hh] = jnp.max(nxt, axis=0, keepdims=True)

    half = tk // 2

    def issue_diagonal(hh, d):
        r0 = d * tk
        q_t = qt_ref[hh, d]
        top = jnp.dot(kaug_ref[hh, r0:r0 + half, :], q_t, preferred_element_type=F32) + mask_ref[:half, :]
        bot = (jnp.dot(kaug_ref[hh, r0 + half:r0 + tk, :], q_t[:, half:], preferred_element_type=F32)
               + mask_ref[half:, half:])
        s_ref[hh, :half, :] = top
        s_ref[hh, half:, half:] = bot
        s_ref[hh, half:, :half] = jnp.full((half, half), NEG_BIG, F32)
        top_max = jnp.max(top, axis=0, keepdims=True)
        smax_ref[hh] = jnp.concatenate(
            [top_max[:, :half], jnp.maximum(top_max[:, half:], jnp.max(bot, axis=0, keepdims=True))], axis=1)

    def consume(hh, qi1, kb1, s, smax):
        m_old = m_ref[hh, qi1]
        m_new = jnp.maximum(m_old, smax)
        p = jnp.exp2(s - m_new).astype(BF16)
        acc_ref[hh, qi1] = (jnp.exp2(m_old - m_new) * acc_ref[hh, qi1]
                            + jnp.dot(vt_ref[hh, kb1], p, preferred_element_type=F32))
        m_ref[hh, qi1] = m_new

    def consume_diagonal(hh, d, s_top, s_bot, smax):
        m_old = m_ref[hh, d]
        m_new = jnp.maximum(m_old, smax)
        p_top = jnp.exp2(s_top - m_new).astype(BF16)
        p_bot = jnp.exp2(s_bot - m_new[:, half:]).astype(BF16)
        v_t = vt_ref[hh, d]
        pv = jnp.dot(v_t[:, :half], p_top, preferred_element_type=F32)
        pv_late = jnp.dot(v_t[:, half:], p_bot, preferred_element_type=F32)
        acc = jnp.exp2(m_old - m_new) * acc_ref[hh, d] + pv
        acc_ref[hh, d] = jnp.concatenate([acc[:, :half], acc[:, half:] + pv_late], axis=1)
        m_ref[hh, d] = m_new

    n_off = n_tiles * (n_tiles - 1) // 2
    zero = jnp.int32(0)
    for hh in range(2):
        issue_diagonal(hh, 0)

    def off_diagonal(carry):
        qi0, kb0, qi1, kb1 = carry
        for hh in range(2):
            s_prev, smax_prev = s_ref[hh], smax_ref[hh]
            issue_scores(hh, qi0, kb0, False)
            consume(hh, qi1, kb1, s_prev, smax_prev)
        wrap = kb0 + 1 == qi0
        return jnp.where(wrap, qi0 + 1, qi0), jnp.where(wrap, 0, kb0 + 1), qi0, kb0

    def trip(_, carry):
        for _ in range(FOX_STEPS_PER_TRIP):
            carry = off_diagonal(carry)
        return carry
    carry = (jnp.int32(1), zero, zero, zero)
    carry = lax.fori_loop(0, n_off // FOX_STEPS_PER_TRIP, trip, carry)
    for _ in range(n_off % FOX_STEPS_PER_TRIP):
        carry = off_diagonal(carry)
    for d in range(1, n_tiles):
        for hh in range(2):
            if d == 1:
                s_prev, smax_prev = s_ref[hh], smax_ref[hh]
                issue_diagonal(hh, d)
                consume(hh, carry[2], carry[3], s_prev, smax_prev)
            else:
                s_top, s_bot, smax_prev = s_ref[hh, :half, :], s_ref[hh, half:, half:], smax_ref[hh]
                issue_diagonal(hh, d)
                consume_diagonal(hh, d - 1, s_top, s_bot, smax_prev)
    for hh in range(2):
        consume_diagonal(hh, n_tiles - 1, s_ref[hh, :half, :], s_ref[hh, half:, half:], smax_ref[hh])

    feat = lax.broadcasted_iota(jnp.int32, (LANES, 1), 0)

    def finish(i, _):
        outs = []
        for hh in range(2):
            acc = acc_ref[hh, i]
            outs.append(acc * (1.0 / acc[one_lane[hh]:one_lane[hh] + 1, :]))
        rows = pl.ds(pl.multiple_of(i * tq, tq), tq)
        o_ref[0, rows, :] = jnp.where(feat < HALF, outs[0], outs[1]).T.astype(o_ref.dtype)
        return 0
    lax.fori_loop(0, n_tiles, finish, 0, unroll=2)


def _fox(qkv, cp, selq, selk, gq2, gk2, e2):
    b, s, _ = qkv.shape
    assert s % PRO_TILE == 0
    npair = B_HEADS // 2
    nblk = s // PRO_TILE
    const = lambda bi, p: (0, 0)
    return pl.pallas_call(
        _fox_kernel,
        grid=(b, npair),
        in_specs=[
            pl.BlockSpec((1, s, LANES), lambda bi, p: (bi, 0, QB_BLK + p)),
            pl.BlockSpec((1, s, LANES), lambda bi, p: (bi, 0, KB_BLK + p)),
            pl.BlockSpec((1, s, LANES), lambda bi, p: (bi, 0, VB_BLK + p)),
            pl.BlockSpec((1, s, LANES), lambda bi, p: (bi, 0, 0)),
            pl.BlockSpec((1, LANES, LANES), lambda bi, p: (p, 0, 0)),
            pl.BlockSpec((1, LANES, LANES), lambda bi, p: (p, 0, 0)),
            pl.BlockSpec((1, LANES), const),
            pl.BlockSpec((1, LANES), const),
            pl.BlockSpec((LANES, LANES), const),
        ],
        out_specs=pl.BlockSpec((1, s, LANES), lambda bi, p: (bi, 0, p)),
        out_shape=jax.ShapeDtypeStruct((b, s, B_HEADS * HEAD_DIM), BF16),
        scratch_shapes=[
            pltpu.VMEM((2, s, LANES), BF16),
            pltpu.VMEM((2, nblk, LANES, PRO_TILE), BF16),
            pltpu.VMEM((2, nblk, LANES, PRO_TILE), BF16),
            pltpu.VMEM((PRO_TILE, PRO_TILE), F32),
            pltpu.VMEM((2, PRO_TILE, PRO_TILE), F32),
            pltpu.VMEM((2, 1, PRO_TILE), F32),
            pltpu.VMEM((2, nblk, 1, PRO_TILE), F32),
            pltpu.VMEM((2, nblk, LANES, PRO_TILE), F32),
        ],
        compiler_params=pltpu.CompilerParams(
            dimension_semantics=("arbitrary", "arbitrary"),
            vmem_limit_bytes=48 * 1024 * 1024),
        name="fox",
    )(qkv, qkv, qkv, cp, selq, selk, gq2, gk2, e2)


def _out_mlp_kernel(x_ref, oa_ref, ob_ref, woa_ref, wob_ref, g_ref, wup_ref, wdn_ref, y_ref):
    h = (x_ref[...]
         + jnp.dot(oa_ref[...], woa_ref[...], preferred_element_type=F32)
         + jnp.dot(ob_ref[...], wob_ref[...], preferred_element_type=F32))
    ms = jnp.sum(h * h, axis=-1, keepdims=True) * (1.0 / D_MODEL)
    hn = (h * lax.rsqrt(ms + EPS) * g_ref[...]).astype(BF16)
    u = jnp.maximum(jnp.dot(hn, wup_ref[...], preferred_element_type=F32), 0.0)
    y_ref[...] = h + jnp.dot((u * u).astype(BF16), wdn_ref[...], preferred_element_type=F32)


def _out_mlp(x2, oa2, ob2, woa, wob, g2, wup, wdn):
    n, d = x2.shape
    tm = TOK_TILE
    const = lambda i: (0, 0)
    resident = dict(pipeline_mode=pl.Buffered(1))
    return pl.pallas_call(
        _out_mlp_kernel,
        grid=(n // tm,),
        in_specs=[
            pl.BlockSpec((tm, d), lambda i: (i, 0)),
            pl.BlockSpec((tm, oa2.shape[1]), lambda i: (i, 0)),
            pl.BlockSpec((tm, ob2.shape[1]), lambda i: (i, 0)),
            pl.BlockSpec(woa.shape, const, **resident),
            pl.BlockSpec(wob.shape, const, **resident),
            pl.BlockSpec((1, d), const),
            pl.BlockSpec(wup.shape, const, **resident),
            pl.BlockSpec(wdn.shape, const, **resident),
        ],
        out_specs=pl.BlockSpec((tm, d), lambda i: (i, 0)),
        out_shape=jax.ShapeDtypeStruct((n, d), F32),
        compiler_params=pltpu.CompilerParams(
            dimension_semantics=("arbitrary",),
            vmem_limit_bytes=52 * 1024 * 1024),
        name="out_mlp",
    )(x2, oa2, ob2, woa, wob, g2, wup, wdn)


def _block_diag_ones():
    i = np.arange(LANES)
    return jnp.asarray((i[:, None] // HALF == i[None, :] // HALF).astype(np.float32), dtype=BF16)


def _fox_select_tables():
    selq = np.zeros((B_HEADS // 2, LANES, LANES), np.float32)
    selk = np.zeros_like(selq)
    for p in range(B_HEADS // 2):
        for hh in range(2):
            h = 2 * p + hh
            base = HALF if hh == 0 else 0
            for piece in range(3):
                selq[p, piece * PIECE_STRIDE + h, base + piece] = 1.0
                selq[p, ONE_LANE, base + 3 + piece] = 1.0
                selk[p, ONE_LANE, base + piece] = 1.0
                selk[p, piece * PIECE_STRIDE + h, base + 3 + piece] = -1.0
    return jnp.asarray(selq, dtype=BF16), jnp.asarray(selk, dtype=BF16)


def _swa_bias_tables():
    slopes = np.exp2(-(8.0 / A_HEADS) * (np.arange(A_HEADS, dtype=np.float32) + 1.0)).astype(np.float32)
    qpos = SUB + np.arange(SUB)
    kpos = np.arange(2 * SUB)
    dist = kpos[:, None] * 0 + qpos[None, :] - kpos[:, None]
    band = (dist >= 0) & (dist < WINDOW)
    out = np.zeros((A_KV_HEADS, 2, 2 * SUB, A_GROUP * SUB), np.float32)
    for g in range(A_KV_HEADS):
        for j in range(A_GROUP):
            alibi = (-slopes[g * A_GROUP + j] * dist.astype(np.float32)).astype(np.float32) * np.float32(LOG2E)
            cols = slice(j * SUB, (j + 1) * SUB)
            out[g, 0, :, cols] = np.where(band, alibi, NEG_BIG)
            out[g, 1, :, cols] = np.where(band & (kpos >= SUB)[:, None], alibi, NEG_BIG)
    return jnp.asarray(out)


def kernel(x, attn_norm_g, w_in, b_forget, q_norm_a, k_norm_a, sink_logits, q_norm_b, k_norm_b, w_out,
           mlp_norm_g, w_up, w_down):
    b, s, d = x.shape
    assert d == D_MODEL and s % Q_TILE == 0 and s % TOK_TILE == 0
    scale = 1.0 / math.sqrt(HEAD_DIM)

    w_pad = jnp.pad(w_in, ((0, 0), (0, PROJ_W - w_in.shape[1]))).astype(BF16)
    bf_pad = jnp.pad(b_forget.astype(F32), (0, LANES - B_HEADS)).reshape(1, LANES)
    e2 = _block_diag_ones()
    selq, selk = _fox_select_tables()
    bias = _swa_bias_tables()
    sink_rows = jnp.repeat(sink_logits.astype(F32).reshape(A_KV_HEADS, 1, A_GROUP) * LOG2E, SUB, axis=2)
    root = math.sqrt(HEAD_DIM)
    gqa2 = (jnp.tile(q_norm_a.astype(F32), 2) * (root * scale * LOG2E)).reshape(1, LANES)
    gka2 = (jnp.tile(k_norm_a.astype(F32), 2) * root).reshape(1, LANES)
    gqb2 = (jnp.tile(q_norm_b.astype(F32), 2) * (root * scale * LOG2E)).reshape(1, LANES)
    gkb2 = (jnp.tile(k_norm_b.astype(F32), 2) * root).reshape(1, LANES)

    qkv, cp = _in_proj(x, attn_norm_g.astype(F32).reshape(1, d), w_pad, bf_pad)
    out_a = _swa(qkv, gqa2, gka2, e2, bias, sink_rows)
    out_b = _fox(qkv, cp, selq, selk, gqb2, gkb2, e2)

    wo = w_out.astype(BF16)
    y = _out_mlp(x.reshape(b * s, d), out_a.reshape(b * s, -1), out_b.reshape(b * s, -1),
                 wo[:A_HEADS * HEAD_DIM], wo[A_HEADS * HEAD_DIM:],
                 mlp_norm_g.astype(F32).reshape(1, d), w_up.astype(BF16), w_down.astype(BF16))
    return y.reshape(b, s, d)
```

```python
import math

import numpy as np
import jax
import jax.numpy as jnp
from jax import lax
from jax.experimental import pallas as pl
from jax.experimental.pallas import tpu as pltpu

F32 = jnp.float32
BF16 = jnp.bfloat16

D_MODEL = 1024
HEAD_DIM = 64
A_HEADS = 8
A_KV_HEADS = 2
A_GROUP = A_HEADS // A_KV_HEADS
B_HEADS = 8
WINDOW = 128
EPS = 1e-6
LOG2E = 1.4426950408889634
NEG_BIG = -1e30

LANES = 128
HALF = LANES // 2

QA_BLK = 0
KA_BLK = 4
VA_BLK = 5
QB_BLK = 6
KB_BLK = 10
VB_BLK = 14
QKV_W = 18 * LANES
PROJ_W = QKV_W + LANES

PIECE_STRIDE = 8
ONE_LANE = 3 * PIECE_STRIDE

PROJ_TILE = 1024
PROJ_PARTS = 2
TOK_TILE = 512
Q_TILE = 512
PRO_TILE = 512
FOX_STEPS_PER_TRIP = 7
SUB = 128

MIB = 1024 * 1024
VMEM_IN_PROJ = 48 * MIB
VMEM_ATTENTION = 48 * MIB
VMEM_OUT_MLP = 52 * MIB


def _lane_iota(shape):
    return lax.broadcasted_iota(jnp.int32, shape, len(shape) - 1)


def _split3(v):
    hi = v.astype(BF16).astype(F32)
    r = v - hi
    mid = r.astype(BF16).astype(F32)
    lo = (r - mid).astype(BF16).astype(F32)
    return hi, mid, lo


def _head_rms_scale(t, e2):
    sq = t * t
    hi = sq.astype(BF16)
    lo = (sq - hi.astype(F32)).astype(BF16)
    ss = (jnp.dot(hi, e2, preferred_element_type=F32) + jnp.dot(lo, e2, preferred_element_type=F32))
    return lax.rsqrt(ss + HEAD_DIM * EPS)


def _in_proj_kernel(x_ref, g_ref, w_ref, bf_ref, qkv_ref, cp_ref, carry_ref):
    @pl.when(pl.program_id(1) == 0)
    def _():
        carry_ref[...] = jnp.zeros_like(carry_ref)

    tm = x_ref.shape[1]
    n_parts = PROJ_PARTS
    parts = [slice(h * (tm // n_parts), (h + 1) * (tm // n_parts)) for h in range(n_parts)]
    xn = []
    for rows in parts:
        x = x_ref[0, rows, :]
        ms = jnp.sum(x * x, axis=-1, keepdims=True) * (1.0 / D_MODEL)
        xn.append((x * lax.rsqrt(ms + EPS) * g_ref[...]).astype(BF16))
    for rows, h in zip(parts[:-1], xn[:-1]):
        qkv_ref[0, rows, :] = jnp.dot(h, w_ref[:, :QKV_W], preferred_element_type=F32).astype(BF16)
    z = jnp.concatenate([jnp.dot(h, w_ref[:, QKV_W:], preferred_element_type=F32) for h in xn], axis=0)
    qkv_ref[0, parts[-1], :] = jnp.dot(xn[-1], w_ref[:, :QKV_W], preferred_element_type=F32).astype(BF16)

    z = z + bf_ref[...]
    lane = _lane_iota(z.shape)
    logf = jnp.minimum(z, 0.0) - jnp.log(1.0 + jnp.exp(-jnp.abs(z)))
    logf = jnp.where(lane < B_HEADS, logf * LOG2E, 0.0)

    row = lax.broadcasted_iota(jnp.int32, z.shape, 0)
    c = logf
    shift = 1
    while shift < tm:
        c = c + jnp.where(row >= shift, pltpu.roll(c, shift, 0), 0.0)
        shift *= 2
    c = c + carry_ref[...]
    carry_ref[...] = c[tm - 1:, :]

    chi, cmid, clo = _split3(c)
    pieces = (chi + pltpu.roll(cmid, PIECE_STRIDE, 1) + pltpu.roll(clo, 2 * PIECE_STRIDE, 1)
              + jnp.where(lane == ONE_LANE, 1.0, 0.0))
    cp_ref[0] = pieces.astype(BF16)


def _in_proj(x, g, w_pad, bf_pad):
    b, s, d = x.shape
    tm = PROJ_TILE
    const = lambda bi, si: (0, 0)
    return pl.pallas_call(
        _in_proj_kernel,
        grid=(b, s // tm),
        in_specs=[
            pl.BlockSpec((1, tm, d), lambda bi, si: (bi, si, 0)),
            pl.BlockSpec((1, d), const),
            pl.BlockSpec((d, PROJ_W), const),
            pl.BlockSpec((1, LANES), const),
        ],
        out_specs=[
            pl.BlockSpec((1, tm, QKV_W), lambda bi, si: (bi, si, 0)),
            pl.BlockSpec((1, tm, LANES), lambda bi, si: (bi, si, 0)),
        ],
        out_shape=[
            jax.ShapeDtypeStruct((b, s, QKV_W), BF16),
            jax.ShapeDtypeStruct((b, s, LANES), BF16),
        ],
        scratch_shapes=[pltpu.VMEM((1, LANES), F32)],
        compiler_params=pltpu.CompilerParams(
            dimension_semantics=("arbitrary", "arbitrary"),
            vmem_limit_bytes=VMEM_IN_PROJ),
        name="in_proj",
    )(x, g, w_pad, bf_pad)


def _swa_kernel(q_ref, kc_ref, kp_ref, vc_ref, vp_ref, gq_ref, gk_ref, e2_ref, bias_ref, sink_ref, o_ref,
                s_ref, smax_ref, p_ref):
    qi = pl.program_id(1)
    tq = q_ref.shape[1]
    e2 = e2_ref[...]
    variant = jnp.where(qi == 0, 1, 0)
    feat = lax.broadcasted_iota(jnp.int32, (LANES, 1), 0)

    k2 = jnp.concatenate([kp_ref[0], kc_ref[0]], axis=0).astype(F32)
    kn = (k2 * _head_rms_scale(k2, e2) * gk_ref[...]).astype(BF16)
    v_t = jnp.concatenate([vp_ref[0], vc_ref[0]], axis=0).T
    one_row = (HALF, 0)
    v_ts = [jnp.where(feat == one_row[g], jnp.ones((), BF16), v_t) for g in range(A_KV_HEADS)]

    zeros = jnp.zeros((HALF, tq), BF16)
    q_ts = []
    for c in range(A_HEADS // 2):
        g = c // (A_GROUP // 2)
        qc = q_ref[0, :, c * LANES:(c + 1) * LANES].astype(F32)
        qn_t = (qc * _head_rms_scale(qc, e2) * gq_ref[...]).astype(BF16).T
        for half in range(2):
            f = qn_t[half * HALF:(half + 1) * HALF]
            q_ts.append(jnp.concatenate([f, zeros] if g == 0 else [zeros, f], axis=0))

    n_sub = tq // SUB
    chains = [(g, r) for g in range(A_KV_HEADS) for r in range(n_sub)]
    for c, (g, r) in enumerate(chains):
        q_t = jnp.concatenate([q_ts[A_GROUP * g + j][:, r * SUB:(r + 1) * SUB] for j in range(A_GROUP)], axis=1)
        s = jnp.dot(kn[r * SUB:r * SUB + 2 * SUB], q_t, preferred_element_type=F32)
        s = s + (bias_ref[g, variant] if r == 0 else bias_ref[g, 0])
        s_ref[c] = s
        smax_ref[c] = jnp.max(s, axis=0, keepdims=True)
    for c, (g, r) in enumerate(chains):
        m = jnp.maximum(smax_ref[c], sink_ref[g])
        p_ref[c] = jnp.exp2(s_ref[c] - m).astype(BF16)
    out_heads = [[] for _ in range(A_HEADS)]
    for c, (g, r) in enumerate(chains):
        sink = sink_ref[g]
        m = jnp.maximum(smax_ref[c], sink)
        pv = jnp.dot(v_ts[g][:, r * SUB:r * SUB + 2 * SUB], p_ref[c], preferred_element_type=F32)
        denom = pv[one_row[g]:one_row[g] + 1, :] + jnp.exp2(sink - m)
        o = pv[g * HALF:(g + 1) * HALF] * (1.0 / denom)
        for j in range(A_GROUP):
            out_heads[A_GROUP * g + j].append(o[:, j * SUB:(j + 1) * SUB])
    for c in range(A_HEADS // 2):
        pair_t = jnp.concatenate([jnp.concatenate(out_heads[2 * c], axis=1),
                                  jnp.concatenate(out_heads[2 * c + 1], axis=1)], axis=0)
        o_ref[0, :, c * LANES:(c + 1) * LANES] = pair_t.astype(o_ref.dtype).T


def _swa(qkv, gq2, gk2, e2, bias_t, sink_rows):
    b, s, _ = qkv.shape
    tq = Q_TILE
    nsub = tq // SUB
    q_w = A_HEADS * HEAD_DIM
    const = lambda bi, qi: (0, 0)
    return pl.pallas_call(
        _swa_kernel,
        grid=(b, s // tq),
        in_specs=[
            pl.BlockSpec((1, tq, q_w), lambda bi, qi: (bi, qi, 0)),
            pl.BlockSpec((1, tq, LANES), lambda bi, qi: (bi, qi, KA_BLK)),
            pl.BlockSpec((1, SUB, LANES), lambda bi, qi: (bi, jnp.maximum(qi * nsub - 1, 0), KA_BLK)),
            pl.BlockSpec((1, tq, LANES), lambda bi, qi: (bi, qi, VA_BLK)),
            pl.BlockSpec((1, SUB, LANES), lambda bi, qi: (bi, jnp.maximum(qi * nsub - 1, 0), VA_BLK)),
            pl.BlockSpec((1, LANES), const),
            pl.BlockSpec((1, LANES), const),
            pl.BlockSpec((LANES, LANES), const),
            pl.BlockSpec(bias_t.shape, lambda bi, qi: (0, 0, 0, 0)),
            pl.BlockSpec(sink_rows.shape, lambda bi, qi: (0, 0, 0)),
        ],
        out_specs=pl.BlockSpec((1, tq, q_w), lambda bi, qi: (bi, qi, 0)),
        out_shape=jax.ShapeDtypeStruct((b, s, q_w), BF16),
        scratch_shapes=[
            pltpu.VMEM((A_KV_HEADS * nsub, 2 * SUB, A_GROUP * SUB), F32),
            pltpu.VMEM((A_KV_HEADS * nsub, 1, A_GROUP * SUB), F32),
            pltpu.VMEM((A_KV_HEADS * nsub, 2 * SUB, A_GROUP * SUB), BF16),
        ],
        compiler_params=pltpu.CompilerParams(
            dimension_semantics=("arbitrary", "arbitrary"),
            vmem_limit_bytes=VMEM_ATTENTION),
        name="swa",
    )(qkv, qkv, qkv, qkv, qkv, gq2, gk2, e2, bias_t, sink_rows)


def _fox_kernel(q_ref, k_ref, v_ref, cp_ref, selq_ref, selk_ref, gq_ref, gk_ref, e2_ref,
                o_ref, kaug_ref, qt_ref, vt_ref, mask_ref, s_ref, smax_ref, m_ref, acc_ref):
    s_len = k_ref.shape[1]
    tq = tk = PRO_TILE
    n_tiles = s_len // tq
    e2 = e2_ref[...]
    lane = _lane_iota((1, LANES))
    low = lane < HALF
    own = (low, jnp.logical_not(low))
    one_lane = (HALF, 0)

    def prepare(i, _):
        rows = pl.ds(pl.multiple_of(i * PRO_TILE, PRO_TILE), PRO_TILE)
        kc = k_ref[0, rows, :].astype(F32)
        kn = kc * _head_rms_scale(kc, e2) * gk_ref[...]
        qc = q_ref[0, rows, :].astype(F32)
        qn = qc * _head_rms_scale(qc, e2) * gq_ref[...]
        vc = v_ref[0, rows, :]
        cp = cp_ref[0, rows, :]
        extra_k = jnp.dot(cp, selk_ref[0], preferred_element_type=F32)
        extra_q = jnp.dot(cp, selq_ref[0], preferred_element_type=F32)
        for hh in range(2):
            kaug_ref[hh, rows, :] = jnp.where(own[hh], kn, extra_k).astype(BF16)
            qt_ref[hh, i] = jnp.where(own[hh], qn, extra_q).astype(BF16).T
            vt_ref[hh, i] = jnp.where(own[hh], vc, jnp.where(lane == one_lane[hh], 1.0, 0.0).astype(BF16)).T
        return 0
    lax.fori_loop(0, n_tiles, prepare, 0, unroll=2)

    @pl.when(jnp.logical_and(pl.program_id(0) == 0, pl.program_id(1) == 0))
    def _build_mask():
        key = lax.broadcasted_iota(jnp.int32, (tk, tq), 0)
        qry = lax.broadcasted_iota(jnp.int32, (tk, tq), 1)
        mask_ref[...] = jnp.where(key <= qry, 0.0, NEG_BIG)

    m_ref[...] = jnp.full_like(m_ref, NEG_BIG)
    acc_ref[...] = jnp.zeros_like(acc_ref)

    def issue_scores(hh, qi, kb, masked):
        krows = pl.ds(pl.multiple_of(kb * tk, tk), tk)
        nxt = jnp.dot(kaug_ref[hh, krows, :], qt_ref[hh, qi], preferred_element_type=F32)
        if masked:
            nxt = nxt + mask_ref[...]
        s_ref[hh] = nxt
        smax_ref[hh] = jnp.max(nxt, axis=0, keepdims=True)

    half = tk // 2

    def issue_diagonal(hh, d):
        r0 = d * tk
        q_t = qt_ref[hh, d]
        top = jnp.dot(kaug_ref[hh, r0:r0 + half, :], q_t, preferred_element_type=F32) + mask_ref[:half, :]
        bot = (jnp.dot(kaug_ref[hh, r0 + half:r0 + tk, :], q_t[:, half:], preferred_element_type=F32)
               + mask_ref[half:, half:])
        s_ref[hh, :half, :] = top
        s_ref[hh, half:, half:] = bot
        s_ref[hh, half:, :half] = jnp.full((half, half), NEG_BIG, F32)
        top_max = jnp.max(top, axis=0, keepdims=True)
        smax_ref[hh] = jnp.concatenate(
            [top_max[:, :half], jnp.maximum(top_max[:, half:], jnp.max(bot, axis=0, keepdims=True))], axis=1)

    def consume(hh, qi1, kb1, s, smax):
        m_old = m_ref[hh, qi1]
        m_new = jnp.maximum(m_old, smax)
        p = jnp.exp2(s - m_new).astype(BF16)
        acc_ref[hh, qi1] = (jnp.exp2(m_old - m_new) * acc_ref[hh, qi1]
                            + jnp.dot(vt_ref[hh, kb1], p, preferred_element_type=F32))
        m_ref[hh, qi1] = m_new

    def consume_diagonal(hh, d, s_top, s_bot, smax):
        m_old = m_ref[hh, d]
        m_new = jnp.maximum(m_old, smax)
        p_top = jnp.exp2(s_top - m_new).astype(BF16)
        p_bot = jnp.exp2(s_bot - m_new[:, half:]).astype(BF16)
        v_t = vt_ref[hh, d]
        pv = jnp.dot(v_t[:, :half], p_top, preferred_element_type=F32)
        pv_late = jnp.dot(v_t[:, half:], p_bot, preferred_element_type=F32)
        acc = jnp.exp2(m_old - m_new) * acc_ref[hh, d] + pv
        acc_ref[hh, d] = jnp.concatenate([acc[:, :half], acc[:, half:] + pv_late], axis=1)
        m_ref[hh, d] = m_new

    n_off = n_tiles * (n_tiles - 1) // 2
    zero = jnp.int32(0)
    for hh in range(2):
        issue_diagonal(hh, 0)

    def off_diagonal(carry):
        qi0, kb0, qi1, kb1 = carry
        for hh in range(2):
            s_prev, smax_prev = s_ref[hh], smax_ref[hh]
            issue_scores(hh, qi0, kb0, False)
            consume(hh, qi1, kb1, s_prev, smax_prev)
        wrap = kb0 + 1 == qi0
        return jnp.where(wrap, qi0 + 1, qi0), jnp.where(wrap, 0, kb0 + 1), qi0, kb0

    def trip(_, carry):
        for _ in range(FOX_STEPS_PER_TRIP):
            carry = off_diagonal(carry)
        return carry
    carry = (jnp.int32(1), zero, zero, zero)
    carry = lax.fori_loop(0, n_off // FOX_STEPS_PER_TRIP, trip, carry)
    for _ in range(n_off % FOX_STEPS_PER_TRIP):
        carry = off_diagonal(carry)
    for d in range(1, n_tiles):
        for hh in range(2):
            if d == 1:
                s_prev, smax_prev = s_ref[hh], smax_ref[hh]
                issue_diagonal(hh, d)
                consume(hh, carry[2], carry[3], s_prev, smax_prev)
            else:
                s_top, s_bot, smax_prev = s_ref[hh, :half, :], s_ref[hh, half:, half:], smax_ref[hh]
                issue_diagonal(hh, d)
                consume_diagonal(hh, d - 1, s_top, s_bot, smax_prev)
    for hh in range(2):
        consume_diagonal(hh, n_tiles - 1, s_ref[hh, :half, :], s_ref[hh, half:, half:], smax_ref[hh])

    feat = lax.broadcasted_iota(jnp.int32, (LANES, 1), 0)

    def finish(i, _):
        outs = []
        for hh in range(2):
            acc = acc_ref[hh, i]
            outs.append(acc * (1.0 / acc[one_lane[hh]:one_lane[hh] + 1, :]))
        rows = pl.ds(pl.multiple_of(i * tq, tq), tq)
        o_ref[0, rows, :] = jnp.where(feat < HALF, outs[0], outs[1]).astype(o_ref.dtype).T
        return 0
    lax.fori_loop(0, n_tiles, finish, 0, unroll=2)


def _fox(qkv, cp, selq, selk, gq2, gk2, e2):
    b, s, _ = qkv.shape
    assert s % PRO_TILE == 0
    npair = B_HEADS // 2
    nblk = s // PRO_TILE
    const = lambda bi, p: (0, 0)
    return pl.pallas_call(
        _fox_kernel,
        grid=(b, npair),
        in_specs=[
            pl.BlockSpec((1, s, LANES), lambda bi, p: (bi, 0, QB_BLK + p)),
            pl.BlockSpec((1, s, LANES), lambda bi, p: (bi, 0, KB_BLK + p)),
            pl.BlockSpec((1, s, LANES), lambda bi, p: (bi, 0, VB_BLK + p)),
            pl.BlockSpec((1, s, LANES), lambda bi, p: (bi, 0, 0)),
            pl.BlockSpec((1, LANES, LANES), lambda bi, p: (p, 0, 0)),
            pl.BlockSpec((1, LANES, LANES), lambda bi, p: (p, 0, 0)),
            pl.BlockSpec((1, LANES), const),
            pl.BlockSpec((1, LANES), const),
            pl.BlockSpec((LANES, LANES), const),
        ],
        out_specs=pl.BlockSpec((1, s, LANES), lambda bi, p: (bi, 0, p)),
        out_shape=jax.ShapeDtypeStruct((b, s, B_HEADS * HEAD_DIM), BF16),
        scratch_shapes=[
            pltpu.VMEM((2, s, LANES), BF16),
            pltpu.VMEM((2, nblk, LANES, PRO_TILE), BF16),
            pltpu.VMEM((2, nblk, LANES, PRO_TILE), BF16),
            pltpu.VMEM((PRO_TILE, PRO_TILE), F32),
            pltpu.VMEM((2, PRO_TILE, PRO_TILE), F32),
            pltpu.VMEM((2, 1, PRO_TILE), F32),
            pltpu.VMEM((2, nblk, 1, PRO_TILE), F32),
            pltpu.VMEM((2, nblk, LANES, PRO_TILE), F32),
        ],
        compiler_params=pltpu.CompilerParams(
            dimension_semantics=("arbitrary", "arbitrary"),
            vmem_limit_bytes=VMEM_ATTENTION),
        name="fox",
    )(qkv, qkv, qkv, cp, selq, selk, gq2, gk2, e2)


def _out_mlp_kernel(x_ref, oa_ref, ob_ref, woa_ref, wob_ref, g_ref, wup_ref, wdn_ref, y_ref):
    h = (x_ref[...]
         + jnp.dot(oa_ref[...], woa_ref[...], preferred_element_type=F32)
         + jnp.dot(ob_ref[...], wob_ref[...], preferred_element_type=F32))
    ms = jnp.sum(h * h, axis=-1, keepdims=True) * (1.0 / D_MODEL)
    hn = (h * lax.rsqrt(ms + EPS) * g_ref[...]).astype(BF16)
    u = jnp.maximum(jnp.dot(hn, wup_ref[...], preferred_element_type=F32), 0.0)
    y_ref[...] = h + jnp.dot((u * u).astype(BF16), wdn_ref[...], preferred_element_type=F32)


def _out_mlp(x2, oa2, ob2, woa, wob, g2, wup, wdn):
    n, d = x2.shape
    tm = TOK_TILE
    const = lambda i: (0, 0)
    resident = dict(pipeline_mode=pl.Buffered(1))
    return pl.pallas_call(
        _out_mlp_kernel,
        grid=(n // tm,),
        in_specs=[
            pl.BlockSpec((tm, d), lambda i: (i, 0)),
            pl.BlockSpec((tm, oa2.shape[1]), lambda i: (i, 0)),
            pl.BlockSpec((tm, ob2.shape[1]), lambda i: (i, 0)),
            pl.BlockSpec(woa.shape, const, **resident),
            pl.BlockSpec(wob.shape, const, **resident),
            pl.BlockSpec((1, d), const),
            pl.BlockSpec(wup.shape, const, **resident),
            pl.BlockSpec(wdn.shape, const, **resident),
        ],
        out_specs=pl.BlockSpec((tm, d), lambda i: (i, 0)),
        out_shape=jax.ShapeDtypeStruct((n, d), F32),
        compiler_params=pltpu.CompilerParams(
            dimension_semantics=("arbitrary",),
            vmem_limit_bytes=VMEM_OUT_MLP),
        name="out_mlp",
    )(x2, oa2, ob2, woa, wob, g2, wup, wdn)


def _block_diag_ones():
    i = np.arange(LANES)
    return jnp.asarray((i[:, None] // HALF == i[None, :] // HALF).astype(np.float32), dtype=BF16)


def _fox_select_tables():
    selq = np.zeros((B_HEADS // 2, LANES, LANES), np.float32)
    selk = np.zeros_like(selq)
    for p in range(B_HEADS // 2):
        for hh in range(2):
            h = 2 * p + hh
            base = HALF if hh == 0 else 0
            for piece in range(3):
                selq[p, piece * PIECE_STRIDE + h, base + piece] = 1.0
                selq[p, ONE_LANE, base + 3 + piece] = 1.0
                selk[p, ONE_LANE, base + piece] = 1.0
                selk[p, piece * PIECE_STRIDE + h, base + 3 + piece] = -1.0
    return jnp.asarray(selq, dtype=BF16), jnp.asarray(selk, dtype=BF16)


def _swa_bias_tables():
    slopes = np.exp2(-(8.0 / A_HEADS) * (np.arange(A_HEADS, dtype=np.float32) + 1.0)).astype(np.float32)
    qpos = SUB + np.arange(SUB)
    kpos = np.arange(2 * SUB)
    dist = kpos[:, None] * 0 + qpos[None, :] - kpos[:, None]
    band = (dist >= 0) & (dist < WINDOW)
    out = np.zeros((A_KV_HEADS, 2, 2 * SUB, A_GROUP * SUB), np.float32)
    for g in range(A_KV_HEADS):
        for j in range(A_GROUP):
            alibi = (-slopes[g * A_GROUP + j] * dist.astype(np.float32)).astype(np.float32) * np.float32(LOG2E)
            cols = slice(j * SUB, (j + 1) * SUB)
            out[g, 0, :, cols] = np.where(band, alibi, NEG_BIG)
            out[g, 1, :, cols] = np.where(band & (kpos >= SUB)[:, None], alibi, NEG_BIG)
    return jnp.asarray(out)


def kernel(x, attn_norm_g, w_in, b_forget, q_norm_a, k_norm_a, sink_logits, q_norm_b, k_norm_b, w_out,
           mlp_norm_g, w_up, w_down):
    b, s, d = x.shape
    assert d == D_MODEL and s % Q_TILE == 0 and s % PROJ_TILE == 0 and (b * s) % TOK_TILE == 0
    scale = 1.0 / math.sqrt(HEAD_DIM)

    w_pad = jnp.pad(w_in, ((0, 0), (0, PROJ_W - w_in.shape[1]))).astype(BF16)
    bf_pad = jnp.pad(b_forget.astype(F32), (0, LANES - B_HEADS)).reshape(1, LANES)
    e2 = _block_diag_ones()
    selq, selk = _fox_select_tables()
    bias = _swa_bias_tables()
    sink_rows = jnp.repeat(sink_logits.astype(F32).reshape(A_KV_HEADS, 1, A_GROUP) * LOG2E, SUB, axis=2)
    root = math.sqrt(HEAD_DIM)
    gqa2 = (jnp.tile(q_norm_a.astype(F32), 2) * (root * scale * LOG2E)).reshape(1, LANES)
    gka2 = (jnp.tile(k_norm_a.astype(F32), 2) * root).reshape(1, LANES)
    gqb2 = (jnp.tile(q_norm_b.astype(F32), 2) * (root * scale * LOG2E)).reshape(1, LANES)
    gkb2 = (jnp.tile(k_norm_b.astype(F32), 2) * root).reshape(1, LANES)

    qkv, cp = _in_proj(x, attn_norm_g.astype(F32).reshape(1, d), w_pad, bf_pad)
    out_a = _swa(qkv, gqa2, gka2, e2, bias, sink_rows)
    out_b = _fox(qkv, cp, selq, selk, gqb2, gkb2, e2)

    wo = w_out.astype(BF16)
    y = _out_mlp(x.reshape(b * s, d), out_a.reshape(b * s, -1), out_b.reshape(b * s, -1),
                 wo[:A_HEADS * HEAD_DIM], wo[A_HEADS * HEAD_DIM:],
                 mlp_norm_g.astype(F32).reshape(1, d), w_up.astype(BF16), w_down.astype(BF16))
    return y.reshape(b, s, d)
```

```python
import math

import numpy as np
import jax
import jax.numpy as jnp
from jax import lax
from jax.experimental import pallas as pl
from jax.experimental.pallas import tpu as pltpu

F32 = jnp.float32
BF16 = jnp.bfloat16

D_MODEL = 1024
HEAD_DIM = 64
A_HEADS = 8
A_KV_HEADS = 2
A_GROUP = A_HEADS // A_KV_HEADS
B_HEADS = 8
WINDOW = 128
EPS = 1e-6
LOG2E = 1.4426950408889634
NEG_BIG = -1e30

LANES = 128
HALF = LANES // 2

QA_BLK = 0
KA_BLK = 4
VA_BLK = 5
QB_BLK = 6
KB_BLK = 10
VB_BLK = 14
QKV_W = 18 * LANES
PROJ_W = QKV_W + LANES

PIECE_STRIDE = 8
ONE_LANE = 3 * PIECE_STRIDE

PROJ_TILE = 1024
PROJ_PARTS = 2
TOK_TILE = 512
Q_TILE = 512
PRO_TILE = 512
FOX_STEPS_PER_TRIP = 14
SUB = 128

MIB = 1024 * 1024
VMEM_IN_PROJ = 48 * MIB
VMEM_ATTENTION = 48 * MIB
VMEM_OUT_MLP = 52 * MIB


def _lane_iota(shape):
    return lax.broadcasted_iota(jnp.int32, shape, len(shape) - 1)


def _split3(v):
    hi = v.astype(BF16).astype(F32)
    r = v - hi
    mid = r.astype(BF16).astype(F32)
    lo = (r - mid).astype(BF16).astype(F32)
    return hi, mid, lo


def _head_rms_scale(t, e2):
    sq = t * t
    hi = sq.astype(BF16)
    lo = (sq - hi.astype(F32)).astype(BF16)
    ss = (jnp.dot(hi, e2, preferred_element_type=F32) + jnp.dot(lo, e2, preferred_element_type=F32))
    return lax.rsqrt(ss + HEAD_DIM * EPS)


def _in_proj_kernel(x_ref, g_ref, w_ref, bf_ref, qkv_ref, cp_ref, carry_ref):
    @pl.when(pl.program_id(1) == 0)
    def _():
        carry_ref[...] = jnp.zeros_like(carry_ref)

    tm = x_ref.shape[1]
    n_parts = PROJ_PARTS
    parts = [slice(h * (tm // n_parts), (h + 1) * (tm // n_parts)) for h in range(n_parts)]
    xn = []
    for rows in parts:
        x = x_ref[0, rows, :]
        ms = jnp.sum(x * x, axis=-1, keepdims=True) * (1.0 / D_MODEL)
        xn.append((x * lax.rsqrt(ms + EPS) * g_ref[...]).astype(BF16))
    for rows, h in zip(parts[:-1], xn[:-1]):
        qkv_ref[0, rows, :] = jnp.dot(h, w_ref[:, :QKV_W], preferred_element_type=F32).astype(BF16)
    z = jnp.concatenate([jnp.dot(h, w_ref[:, QKV_W:], preferred_element_type=F32) for h in xn], axis=0)
    qkv_ref[0, parts[-1], :] = jnp.dot(xn[-1], w_ref[:, :QKV_W], preferred_element_type=F32).astype(BF16)

    z = z + bf_ref[...]
    lane = _lane_iota(z.shape)
    logf = jnp.minimum(z, 0.0) - jnp.log(1.0 + jnp.exp(-jnp.abs(z)))
    logf = jnp.where(lane < B_HEADS, logf * LOG2E, 0.0)

    row = lax.broadcasted_iota(jnp.int32, z.shape, 0)
    c = logf
    shift = 1
    while shift < tm:
        c = c + jnp.where(row >= shift, pltpu.roll(c, shift, 0), 0.0)
        shift *= 2
    c = c + carry_ref[...]
    carry_ref[...] = c[tm - 1:, :]

    chi, cmid, clo = _split3(c)
    pieces = (chi + pltpu.roll(cmid, PIECE_STRIDE, 1) + pltpu.roll(clo, 2 * PIECE_STRIDE, 1)
              + jnp.where(lane == ONE_LANE, 1.0, 0.0))
    cp_ref[0] = pieces.astype(BF16)


def _in_proj(x, g, w_pad, bf_pad):
    b, s, d = x.shape
    tm = PROJ_TILE
    const = lambda bi, si: (0, 0)
    return pl.pallas_call(
        _in_proj_kernel,
        grid=(b, s // tm),
        in_specs=[
            pl.BlockSpec((1, tm, d), lambda bi, si: (bi, si, 0)),
            pl.BlockSpec((1, d), const),
            pl.BlockSpec((d, PROJ_W), const),
            pl.BlockSpec((1, LANES), const),
        ],
        out_specs=[
            pl.BlockSpec((1, tm, QKV_W), lambda bi, si: (bi, si, 0)),
            pl.BlockSpec((1, tm, LANES), lambda bi, si: (bi, si, 0)),
        ],
        out_shape=[
            jax.ShapeDtypeStruct((b, s, QKV_W), BF16),
            jax.ShapeDtypeStruct((b, s, LANES), BF16),
        ],
        scratch_shapes=[pltpu.VMEM((1, LANES), F32)],
        compiler_params=pltpu.CompilerParams(
            dimension_semantics=("arbitrary", "arbitrary"),
            vmem_limit_bytes=VMEM_IN_PROJ),
        name="in_proj",
    )(x, g, w_pad, bf_pad)


def _swa_kernel(q_ref, kc_ref, kp_ref, vc_ref, vp_ref, gq_ref, gk_ref, e2_ref, bias_ref, sink_ref, o_ref,
                s_ref, smax_ref, p_ref):
    qi = pl.program_id(1)
    tq = q_ref.shape[1]
    e2 = e2_ref[...]
    variant = jnp.where(qi == 0, 1, 0)
    feat = lax.broadcasted_iota(jnp.int32, (LANES, 1), 0)

    k2 = jnp.concatenate([kp_ref[0], kc_ref[0]], axis=0).astype(F32)
    kn = (k2 * _head_rms_scale(k2, e2) * gk_ref[...]).astype(BF16)
    v_t = jnp.concatenate([vp_ref[0], vc_ref[0]], axis=0).T
    one_row = (HALF, 0)
    v_ts = [jnp.where(feat == one_row[g], jnp.ones((), BF16), v_t) for g in range(A_KV_HEADS)]

    zeros = jnp.zeros((HALF, tq), BF16)
    q_ts = []
    for c in range(A_HEADS // 2):
        g = c // (A_GROUP // 2)
        qc = q_ref[0, :, c * LANES:(c + 1) * LANES].astype(F32)
        qn_t = (qc * _head_rms_scale(qc, e2) * gq_ref[...]).astype(BF16).T
        for half in range(2):
            f = qn_t[half * HALF:(half + 1) * HALF]
            q_ts.append(jnp.concatenate([f, zeros] if g == 0 else [zeros, f], axis=0))

    n_sub = tq // SUB
    chains = [(g, r) for g in range(A_KV_HEADS) for r in range(n_sub)]
    for c, (g, r) in enumerate(chains):
        q_t = jnp.concatenate([q_ts[A_GROUP * g + j][:, r * SUB:(r + 1) * SUB] for j in range(A_GROUP)], axis=1)
        s = jnp.dot(kn[r * SUB:r * SUB + 2 * SUB], q_t, preferred_element_type=F32)
        s = s + (bias_ref[g, variant] if r == 0 else bias_ref[g, 0])
        s_ref[c] = s
        smax_ref[c] = jnp.max(s, axis=0, keepdims=True)
    for c, (g, r) in enumerate(chains):
        m = jnp.maximum(smax_ref[c], sink_ref[g])
        p_ref[c] = jnp.exp2(s_ref[c] - m).astype(BF16)
    out_heads = [[] for _ in range(A_HEADS)]
    for c, (g, r) in enumerate(chains):
        sink = sink_ref[g]
        m = jnp.maximum(smax_ref[c], sink)
        pv = jnp.dot(v_ts[g][:, r * SUB:r * SUB + 2 * SUB], p_ref[c], preferred_element_type=F32)
        denom = pv[one_row[g]:one_row[g] + 1, :] + jnp.exp2(sink - m)
        o = pv[g * HALF:(g + 1) * HALF] * (1.0 / denom)
        for j in range(A_GROUP):
            out_heads[A_GROUP * g + j].append(o[:, j * SUB:(j + 1) * SUB])
    for c in range(A_HEADS // 2):
        pair_t = jnp.concatenate([jnp.concatenate(out_heads[2 * c], axis=1),
                                  jnp.concatenate(out_heads[2 * c + 1], axis=1)], axis=0)
        o_ref[0, :, c * LANES:(c + 1) * LANES] = pair_t.astype(o_ref.dtype).T


def _swa(qkv, gq2, gk2, e2, bias_t, sink_rows):
    b, s, _ = qkv.shape
    tq = Q_TILE
    nsub = tq // SUB
    q_w = A_HEADS * HEAD_DIM
    const = lambda bi, qi: (0, 0)
    return pl.pallas_call(
        _swa_kernel,
        grid=(b, s // tq),
        in_specs=[
            pl.BlockSpec((1, tq, q_w), lambda bi, qi: (bi, qi, 0)),
            pl.BlockSpec((1, tq, LANES), lambda bi, qi: (bi, qi, KA_BLK)),
            pl.BlockSpec((1, SUB, LANES), lambda bi, qi: (bi, jnp.maximum(qi * nsub - 1, 0), KA_BLK)),
            pl.BlockSpec((1, tq, LANES), lambda bi, qi: (bi, qi, VA_BLK)),
            pl.BlockSpec((1, SUB, LANES), lambda bi, qi: (bi, jnp.maximum(qi * nsub - 1, 0), VA_BLK)),
            pl.BlockSpec((1, LANES), const),
            pl.BlockSpec((1, LANES), const),
            pl.BlockSpec((LANES, LANES), const),
            pl.BlockSpec(bias_t.shape, lambda bi, qi: (0, 0, 0, 0)),
            pl.BlockSpec(sink_rows.shape, lambda bi, qi: (0, 0, 0)),
        ],
        out_specs=pl.BlockSpec((1, tq, q_w), lambda bi, qi: (bi, qi, 0)),
        out_shape=jax.ShapeDtypeStruct((b, s, q_w), BF16),
        scratch_shapes=[
            pltpu.VMEM((A_KV_HEADS * nsub, 2 * SUB, A_GROUP * SUB), F32),
            pltpu.VMEM((A_KV_HEADS * nsub, 1, A_GROUP * SUB), F32),
            pltpu.VMEM((A_KV_HEADS * nsub, 2 * SUB, A_GROUP * SUB), BF16),
        ],
        compiler_params=pltpu.CompilerParams(
            dimension_semantics=("arbitrary", "arbitrary"),
            vmem_limit_bytes=VMEM_ATTENTION),
        name="swa",
    )(qkv, qkv, qkv, qkv, qkv, gq2, gk2, e2, bias_t, sink_rows)


def _fox_kernel(q_ref, k_ref, v_ref, cp_ref, selq_ref, selk_ref, gq_ref, gk_ref, e2_ref,
                o_ref, kaug_ref, qt_ref, vt_ref, mask_ref, s_ref, smax_ref, m_ref, acc_ref):
    s_len = k_ref.shape[1]
    tq = tk = PRO_TILE
    n_tiles = s_len // tq
    e2 = e2_ref[...]
    lane = _lane_iota((1, LANES))
    low = lane < HALF
    own = (low, jnp.logical_not(low))
    one_lane = (HALF, 0)

    def prepare(i, _):
        rows = pl.ds(pl.multiple_of(i * PRO_TILE, PRO_TILE), PRO_TILE)
        kc = k_ref[0, rows, :].astype(F32)
        kn = kc * _head_rms_scale(kc, e2) * gk_ref[...]
        qc = q_ref[0, rows, :].astype(F32)
        qn = qc * _head_rms_scale(qc, e2) * gq_ref[...]
        vc = v_ref[0, rows, :]
        cp = cp_ref[0, rows, :]
        extra_k = jnp.dot(cp, selk_ref[0], preferred_element_type=F32)
        extra_q = jnp.dot(cp, selq_ref[0], preferred_element_type=F32)
        for hh in range(2):
            kaug_ref[hh, rows, :] = jnp.where(own[hh], kn, extra_k).astype(BF16)
            qt_ref[hh, i] = jnp.where(own[hh], qn, extra_q).astype(BF16).T
            vt_ref[hh, i] = jnp.where(own[hh], vc, jnp.where(lane == one_lane[hh], 1.0, 0.0).astype(BF16)).T
        return 0
    lax.fori_loop(0, n_tiles, prepare, 0, unroll=True)

    @pl.when(jnp.logical_and(pl.program_id(0) == 0, pl.program_id(1) == 0))
    def _build_mask():
        key = lax.broadcasted_iota(jnp.int32, (tk, tq), 0)
        qry = lax.broadcasted_iota(jnp.int32, (tk, tq), 1)
        mask_ref[...] = jnp.where(key <= qry, 0.0, NEG_BIG)

    m_ref[...] = jnp.full_like(m_ref, NEG_BIG)
    acc_ref[...] = jnp.zeros_like(acc_ref)

    def issue_scores(hh, qi, kb, masked):
        krows = pl.ds(pl.multiple_of(kb * tk, tk), tk)
        nxt = jnp.dot(kaug_ref[hh, krows, :], qt_ref[hh, qi], preferred_element_type=F32)
        if masked:
            nxt = nxt + mask_ref[...]
        s_ref[hh] = nxt
        smax_ref[hh] = jnp.max(nxt, axis=0, keepdims=True)

    half = tk // 2

    def issue_diagonal(hh, d):
        r0 = d * tk
        q_t = qt_ref[hh, d]
        top = jnp.dot(kaug_ref[hh, r0:r0 + half, :], q_t, preferred_element_type=F32) + mask_ref[:half, :]
        bot = (jnp.dot(kaug_ref[hh, r0 + half:r0 + tk, :], q_t[:, half:], preferred_element_type=F32)
               + mask_ref[half:, half:])
        s_ref[hh, :half, :] = top
        s_ref[hh, half:, half:] = bot
        s_ref[hh, half:, :half] = jnp.full((half, half), NEG_BIG, F32)
        top_max = jnp.max(top, axis=0, keepdims=True)
        smax_ref[hh] = jnp.concatenate(
            [top_max[:, :half], jnp.maximum(top_max[:, half:], jnp.max(bot, axis=0, keepdims=True))], axis=1)

    def consume(hh, qi1, kb1, s, smax):
        m_old = m_ref[hh, qi1]
        m_new = jnp.maximum(m_old, smax)
        p = jnp.exp2(s - m_new).astype(BF16)
        acc_ref[hh, qi1] = (jnp.exp2(m_old - m_new) * acc_ref[hh, qi1]
                            + jnp.dot(vt_ref[hh, kb1], p, preferred_element_type=F32))
        m_ref[hh, qi1] = m_new

    def consume_diagonal(hh, d, s_top, s_bot, smax):
        m_old = m_ref[hh, d]
        m_new = jnp.maximum(m_old, smax)
        p_top = jnp.exp2(s_top - m_new).astype(BF16)
        p_bot = jnp.exp2(s_bot - m_new[:, half:]).astype(BF16)
        v_t = vt_ref[hh, d]
        pv = jnp.dot(v_t[:, :half], p_top, preferred_element_type=F32)
        pv_late = jnp.dot(v_t[:, half:], p_bot, preferred_element_type=F32)
        acc = jnp.exp2(m_old - m_new) * acc_ref[hh, d] + pv
        acc_ref[hh, d] = jnp.concatenate([acc[:, :half], acc[:, half:] + pv_late], axis=1)
        m_ref[hh, d] = m_new

    n_off = n_tiles * (n_tiles - 1) // 2
    zero = jnp.int32(0)
    for hh in range(2):
        issue_diagonal(hh, 0)

    def off_diagonal(carry):
        qi0, kb0, qi1, kb1 = carry
        for hh in range(2):
            s_prev, smax_prev = s_ref[hh], smax_ref[hh]
            issue_scores(hh, qi0, kb0, False)
            consume(hh, qi1, kb1, s_prev, smax_prev)
        wrap = kb0 + 1 == qi0
        return jnp.where(wrap, qi0 + 1, qi0), jnp.where(wrap, 0, kb0 + 1), qi0, kb0

    def trip(_, carry):
        for _ in range(FOX_STEPS_PER_TRIP):
            carry = off_diagonal(carry)
        return carry
    carry = (jnp.int32(1), zero, zero, zero)
    carry = lax.fori_loop(0, n_off // FOX_STEPS_PER_TRIP, trip, carry)
    for _ in range(n_off % FOX_STEPS_PER_TRIP):
        carry = off_diagonal(carry)
    for d in range(1, n_tiles):
        for hh in range(2):
            if d == 1:
                s_prev, smax_prev = s_ref[hh], smax_ref[hh]
                issue_diagonal(hh, d)
                consume(hh, carry[2], carry[3], s_prev, smax_prev)
            else:
                s_top, s_bot, smax_prev = s_ref[hh, :half, :], s_ref[hh, half:, half:], smax_ref[hh]
                issue_diagonal(hh, d)
                consume_diagonal(hh, d - 1, s_top, s_bot, smax_prev)
    for hh in range(2):
        consume_diagonal(hh, n_tiles - 1, s_ref[hh, :half, :], s_ref[hh, half:, half:], smax_ref[hh])

    feat = lax.broadcasted_iota(jnp.int32, (LANES, 1), 0)

    def finish(i, _):
        outs = []
        for hh in range(2):
            acc = acc_ref[hh, i]
            outs.append(acc * (1.0 / acc[one_lane[hh]:one_lane[hh] + 1, :]))
        rows = pl.ds(pl.multiple_of(i * tq, tq), tq)
        o_ref[0, rows, :] = jnp.where(feat < HALF, outs[0], outs[1]).astype(o_ref.dtype).T
        return 0
    lax.fori_loop(0, n_tiles, finish, 0, unroll=True)


def _fox(qkv, cp, selq, selk, gq2, gk2, e2):
    b, s, _ = qkv.shape
    assert s % PRO_TILE == 0
    npair = B_HEADS // 2
    nblk = s // PRO_TILE
    const = lambda bi, p: (0, 0)
    return pl.pallas_call(
        _fox_kernel,
        grid=(b, npair),
        in_specs=[
            pl.BlockSpec((1, s, LANES), lambda bi, p: (bi, 0, QB_BLK + p)),
            pl.BlockSpec((1, s, LANES), lambda bi, p: (bi, 0, KB_BLK + p)),
            pl.BlockSpec((1, s, LANES), lambda bi, p: (bi, 0, VB_BLK + p)),
            pl.BlockSpec((1, s, LANES), lambda bi, p: (bi, 0, 0)),
            pl.BlockSpec((1, LANES, LANES), lambda bi, p: (p, 0, 0)),
            pl.BlockSpec((1, LANES, LANES), lambda bi, p: (p, 0, 0)),
            pl.BlockSpec((1, LANES), const),
            pl.BlockSpec((1, LANES), const),
            pl.BlockSpec((LANES, LANES), const),
        ],
        out_specs=pl.BlockSpec((1, s, LANES), lambda bi, p: (bi, 0, p)),
        out_shape=jax.ShapeDtypeStruct((b, s, B_HEADS * HEAD_DIM), BF16),
        scratch_shapes=[
            pltpu.VMEM((2, s, LANES), BF16),
            pltpu.VMEM((2, nblk, LANES, PRO_TILE), BF16),
            pltpu.VMEM((2, nblk, LANES, PRO_TILE), BF16),
            pltpu.VMEM((PRO_TILE, PRO_TILE), F32),
            pltpu.VMEM((2, PRO_TILE, PRO_TILE), F32),
            pltpu.VMEM((2, 1, PRO_TILE), F32),
            pltpu.VMEM((2, nblk, 1, PRO_TILE), F32),
            pltpu.VMEM((2, nblk, LANES, PRO_TILE), F32),
        ],
        compiler_params=pltpu.CompilerParams(
            dimension_semantics=("arbitrary", "arbitrary"),
            vmem_limit_bytes=VMEM_ATTENTION),
        name="fox",
    )(qkv, qkv, qkv, cp, selq, selk, gq2, gk2, e2)


def _out_mlp_kernel(x_ref, oa_ref, ob_ref, woa_ref, wob_ref, g_ref, wup_ref, wdn_ref, y_ref):
    h = (x_ref[...]
         + jnp.dot(oa_ref[...], woa_ref[...], preferred_element_type=F32)
         + jnp.dot(ob_ref[...], wob_ref[...], preferred_element_type=F32))
    ms = jnp.sum(h * h, axis=-1, keepdims=True) * (1.0 / D_MODEL)
    hn = (h * lax.rsqrt(ms + EPS) * g_ref[...]).astype(BF16)
    u = jnp.maximum(jnp.dot(hn, wup_ref[...], preferred_element_type=F32), 0.0)
    y_ref[...] = h + jnp.dot((u * u).astype(BF16), wdn_ref[...], preferred_element_type=F32)


def _out_mlp(x2, oa2, ob2, woa, wob, g2, wup, wdn):
    n, d = x2.shape
    tm = TOK_TILE
    const = lambda i: (0, 0)
    resident = dict(pipeline_mode=pl.Buffered(1))
    return pl.pallas_call(
        _out_mlp_kernel,
        grid=(n // tm,),
        in_specs=[
            pl.BlockSpec((tm, d), lambda i: (i, 0)),
            pl.BlockSpec((tm, oa2.shape[1]), lambda i: (i, 0)),
            pl.BlockSpec((tm, ob2.shape[1]), lambda i: (i, 0)),
            pl.BlockSpec(woa.shape, const, **resident),
            pl.BlockSpec(wob.shape, const, **resident),
            pl.BlockSpec((1, d), const),
            pl.BlockSpec(wup.shape, const, **resident),
            pl.BlockSpec(wdn.shape, const, **resident),
        ],
        out_specs=pl.BlockSpec((tm, d), lambda i: (i, 0)),
        out_shape=jax.ShapeDtypeStruct((n, d), F32),
        compiler_params=pltpu.CompilerParams(
            dimension_semantics=("arbitrary",),
            vmem_limit_bytes=VMEM_OUT_MLP),
        name="out_mlp",
    )(x2, oa2, ob2, woa, wob, g2, wup, wdn)


def _block_diag_ones():
    i = np.arange(LANES)
    return jnp.asarray((i[:, None] // HALF == i[None, :] // HALF).astype(np.float32), dtype=BF16)


def _fox_select_tables():
    selq = np.zeros((B_HEADS // 2, LANES, LANES), np.float32)
    selk = np.zeros_like(selq)
    for p in range(B_HEADS // 2):
        for hh in range(2):
            h = 2 * p + hh
            base = HALF if hh == 0 else 0
            for piece in range(3):
                selq[p, piece * PIECE_STRIDE + h, base + piece] = 1.0
                selq[p, ONE_LANE, base + 3 + piece] = 1.0
                selk[p, ONE_LANE, base + piece] = 1.0
                selk[p, piece * PIECE_STRIDE + h, base + 3 + piece] = -1.0
    return jnp.asarray(selq, dtype=BF16), jnp.asarray(selk, dtype=BF16)


def _swa_bias_tables():
    slopes = np.exp2(-(8.0 / A_HEADS) * (np.arange(A_HEADS, dtype=np.float32) + 1.0)).astype(np.float32)
    qpos = SUB + np.arange(SUB)
    kpos = np.arange(2 * SUB)
    dist = kpos[:, None] * 0 + qpos[None, :] - kpos[:, None]
    band = (dist >= 0) & (dist < WINDOW)
    out = np.zeros((A_KV_HEADS, 2, 2 * SUB, A_GROUP * SUB), np.float32)
    for g in range(A_KV_HEADS):
        for j in range(A_GROUP):
            alibi = (-slopes[g * A_GROUP + j] * dist.astype(np.float32)).astype(np.float32) * np.float32(LOG2E)
            cols = slice(j * SUB, (j + 1) * SUB)
            out[g, 0, :, cols] = np.where(band, alibi, NEG_BIG)
            out[g, 1, :, cols] = np.where(band & (kpos >= SUB)[:, None], alibi, NEG_BIG)
    return jnp.asarray(out)


def kernel(x, attn_norm_g, w_in, b_forget, q_norm_a, k_norm_a, sink_logits, q_norm_b, k_norm_b, w_out,
           mlp_norm_g, w_up, w_down):
    b, s, d = x.shape
    assert d == D_MODEL and s % Q_TILE == 0 and s % PROJ_TILE == 0 and (b * s) % TOK_TILE == 0
    scale = 1.0 / math.sqrt(HEAD_DIM)

    w_pad = jnp.pad(w_in, ((0, 0), (0, PROJ_W - w_in.shape[1]))).astype(BF16)
    bf_pad = jnp.pad(b_forget.astype(F32), (0, LANES - B_HEADS)).reshape(1, LANES)
    e2 = _block_diag_ones()
    selq, selk = _fox_select_tables()
    bias = _swa_bias_tables()
    sink_rows = jnp.repeat(sink_logits.astype(F32).reshape(A_KV_HEADS, 1, A_GROUP) * LOG2E, SUB, axis=2)
    root = math.sqrt(HEAD_DIM)
    gqa2 = (jnp.tile(q_norm_a.astype(F32), 2) * (root * scale * LOG2E)).reshape(1, LANES)
    gka2 = (jnp.tile(k_norm_a.astype(F32), 2) * root).reshape(1, LANES)
    gqb2 = (jnp.tile(q_norm_b.astype(F32), 2) * (root * scale * LOG2E)).reshape(1, LANES)
    gkb2 = (jnp.tile(k_norm_b.astype(F32), 2) * root).reshape(1, LANES)

    qkv, cp = _in_proj(x, attn_norm_g.astype(F32).reshape(1, d), w_pad, bf_pad)
    out_a = _swa(qkv, gqa2, gka2, e2, bias, sink_rows)
    out_b = _fox(qkv, cp, selq, selk, gqb2, gkb2, e2)

    wo = w_out.astype(BF16)
    y = _out_mlp(x.reshape(b * s, d), out_a.reshape(b * s, -1), out_b.reshape(b * s, -1),
                 wo[:A_HEADS * HEAD_DIM], wo[A_HEADS * HEAD_DIM:],
                 mlp_norm_g.astype(F32).reshape(1, d), w_up.astype(BF16), w_down.astype(BF16))
    return y.reshape(b, s, d)
```

```python
import math

import numpy as np
import jax
import jax.numpy as jnp
from jax import lax
from jax.experimental import pallas as pl
from jax.experimental.pallas import tpu as pltpu

F32 = jnp.float32
BF16 = jnp.bfloat16

D_MODEL = 1024
HEAD_DIM = 64
A_HEADS = 8
A_KV_HEADS = 2
A_GROUP = A_HEADS // A_KV_HEADS
B_HEADS = 8
WINDOW = 128
EPS = 1e-6
LOG2E = 1.4426950408889634
NEG_BIG = -1e30

LANES = 128
HALF = LANES // 2

QA_BLK = 0
KA_BLK = 4
VA_BLK = 5
QB_BLK = 6
KB_BLK = 10
VB_BLK = 14
QKV_W = 18 * LANES
PROJ_W = QKV_W + LANES

PIECE_STRIDE = 8
ONE_LANE = 3 * PIECE_STRIDE

PROJ_TILE = 1024
PROJ_PARTS = 2
TOK_TILE = 512
Q_TILE = 1024
PRO_TILE = 512
FOX_STEPS_PER_TRIP = 14
SUB = 128

MIB = 1024 * 1024
VMEM_IN_PROJ = 48 * MIB
VMEM_ATTENTION = 48 * MIB
VMEM_OUT_MLP = 52 * MIB


def _lane_iota(shape):
    return lax.broadcasted_iota(jnp.int32, shape, len(shape) - 1)


def _split3(v):
    hi = v.astype(BF16).astype(F32)
    r = v - hi
    mid = r.astype(BF16).astype(F32)
    lo = (r - mid).astype(BF16).astype(F32)
    return hi, mid, lo


def _head_rms_scale(t, e2):
    sq = t * t
    hi = sq.astype(BF16)
    lo = (sq - hi.astype(F32)).astype(BF16)
    ss = (jnp.dot(hi, e2, preferred_element_type=F32) + jnp.dot(lo, e2, preferred_element_type=F32))
    return lax.rsqrt(ss + HEAD_DIM * EPS)


def _in_proj_kernel(x_ref, g_ref, w_ref, bf_ref, qkv_ref, cp_ref, carry_ref):
    @pl.when(pl.program_id(1) == 0)
    def _():
        carry_ref[...] = jnp.zeros_like(carry_ref)

    tm = x_ref.shape[1]
    n_parts = PROJ_PARTS
    parts = [slice(h * (tm // n_parts), (h + 1) * (tm // n_parts)) for h in range(n_parts)]
    xn = []
    for rows in parts:
        x = x_ref[0, rows, :]
        ms = jnp.sum(x * x, axis=-1, keepdims=True) * (1.0 / D_MODEL)
        xn.append((x * lax.rsqrt(ms + EPS) * g_ref[...]).astype(BF16))
    for rows, h in zip(parts[:-1], xn[:-1]):
        qkv_ref[0, rows, :] = jnp.dot(h, w_ref[:, :QKV_W], preferred_element_type=F32).astype(BF16)
    z = jnp.concatenate([jnp.dot(h, w_ref[:, QKV_W:], preferred_element_type=F32) for h in xn], axis=0)
    qkv_ref[0, parts[-1], :] = jnp.dot(xn[-1], w_ref[:, :QKV_W], preferred_element_type=F32).astype(BF16)

    z = z + bf_ref[...]
    lane = _lane_iota(z.shape)
    logf = jnp.minimum(z, 0.0) - jnp.log(1.0 + jnp.exp(-jnp.abs(z)))
    logf = jnp.where(lane < B_HEADS, logf * LOG2E, 0.0)

    row = lax.broadcasted_iota(jnp.int32, z.shape, 0)
    c = logf
    shift = 1
    while shift < tm:
        c = c + jnp.where(row >= shift, pltpu.roll(c, shift, 0), 0.0)
        shift *= 2
    c = c + carry_ref[...]
    carry_ref[...] = c[tm - 1:, :]

    chi, cmid, clo = _split3(c)
    pieces = (chi + pltpu.roll(cmid, PIECE_STRIDE, 1) + pltpu.roll(clo, 2 * PIECE_STRIDE, 1)
              + jnp.where(lane == ONE_LANE, 1.0, 0.0))
    cp_ref[0] = pieces.astype(BF16)


def _in_proj(x, g, w_pad, bf_pad):
    b, s, d = x.shape
    tm = PROJ_TILE
    const = lambda bi, si: (0, 0)
    return pl.pallas_call(
        _in_proj_kernel,
        grid=(b, s // tm),
        in_specs=[
            pl.BlockSpec((1, tm, d), lambda bi, si: (bi, si, 0)),
            pl.BlockSpec((1, d), const),
            pl.BlockSpec((d, PROJ_W), const),
            pl.BlockSpec((1, LANES), const),
        ],
        out_specs=[
            pl.BlockSpec((1, tm, QKV_W), lambda bi, si: (bi, si, 0)),
            pl.BlockSpec((1, tm, LANES), lambda bi, si: (bi, si, 0)),
        ],
        out_shape=[
            jax.ShapeDtypeStruct((b, s, QKV_W), BF16),
            jax.ShapeDtypeStruct((b, s, LANES), BF16),
        ],
        scratch_shapes=[pltpu.VMEM((1, LANES), F32)],
        compiler_params=pltpu.CompilerParams(
            dimension_semantics=("arbitrary", "arbitrary"),
            vmem_limit_bytes=VMEM_IN_PROJ),
        name="in_proj",
    )(x, g, w_pad, bf_pad)


def _swa_kernel(q_ref, kc_ref, kp_ref, vc_ref, vp_ref, gq_ref, gk_ref, e2_ref, bias_ref, sink_ref, o_ref,
                s_ref, smax_ref, p_ref):
    qi = pl.program_id(1)
    tq = q_ref.shape[1]
    e2 = e2_ref[...]
    variant = jnp.where(qi == 0, 1, 0)
    feat = lax.broadcasted_iota(jnp.int32, (LANES, 1), 0)

    k2 = jnp.concatenate([kp_ref[0], kc_ref[0]], axis=0).astype(F32)
    kn = (k2 * _head_rms_scale(k2, e2) * gk_ref[...]).astype(BF16)
    v_t = jnp.concatenate([vp_ref[0], vc_ref[0]], axis=0).T
    one_row = (HALF, 0)
    v_ts = [jnp.where(feat == one_row[g], jnp.ones((), BF16), v_t) for g in range(A_KV_HEADS)]

    zeros = jnp.zeros((HALF, tq), BF16)
    q_ts = []
    for c in range(A_HEADS // 2):
        g = c // (A_GROUP // 2)
        qc = q_ref[0, :, c * LANES:(c + 1) * LANES].astype(F32)
        qn_t = (qc * _head_rms_scale(qc, e2) * gq_ref[...]).astype(BF16).T
        for half in range(2):
            f = qn_t[half * HALF:(half + 1) * HALF]
            q_ts.append(jnp.concatenate([f, zeros] if g == 0 else [zeros, f], axis=0))

    n_sub = tq // SUB
    chains = [(g, r) for g in range(A_KV_HEADS) for r in range(n_sub)]
    for c, (g, r) in enumerate(chains):
        q_t = jnp.concatenate([q_ts[A_GROUP * g + j][:, r * SUB:(r + 1) * SUB] for j in range(A_GROUP)], axis=1)
        s = jnp.dot(kn[r * SUB:r * SUB + 2 * SUB], q_t, preferred_element_type=F32)
        s = s + (bias_ref[g, variant] if r == 0 else bias_ref[g, 0])
        s_ref[c] = s
        smax_ref[c] = jnp.max(s, axis=0, keepdims=True)
    for c, (g, r) in enumerate(chains):
        m = jnp.maximum(smax_ref[c], sink_ref[g])
        p_ref[c] = jnp.exp2(s_ref[c] - m).astype(BF16)
    out_heads = [[] for _ in range(A_HEADS)]
    for c, (g, r) in enumerate(chains):
        sink = sink_ref[g]
        m = jnp.maximum(smax_ref[c], sink)
        pv = jnp.dot(v_ts[g][:, r * SUB:r * SUB + 2 * SUB], p_ref[c], preferred_element_type=F32)
        denom = pv[one_row[g]:one_row[g] + 1, :] + jnp.exp2(sink - m)
        o = pv[g * HALF:(g + 1) * HALF] * (1.0 / denom)
        for j in range(A_GROUP):
            out_heads[A_GROUP * g + j].append(o[:, j * SUB:(j + 1) * SUB])
    for c in range(A_HEADS // 2):
        pair_t = jnp.concatenate([jnp.concatenate(out_heads[2 * c], axis=1),
                                  jnp.concatenate(out_heads[2 * c + 1], axis=1)], axis=0)
        o_ref[0, :, c * LANES:(c + 1) * LANES] = pair_t.astype(o_ref.dtype).T


def _swa(qkv, gq2, gk2, e2, bias_t, sink_rows):
    b, s, _ = qkv.shape
    tq = Q_TILE
    nsub = tq // SUB
    q_w = A_HEADS * HEAD_DIM
    const = lambda bi, qi: (0, 0)
    return pl.pallas_call(
        _swa_kernel,
        grid=(b, s // tq),
        in_specs=[
            pl.BlockSpec((1, tq, q_w), lambda bi, qi: (bi, qi, 0)),
            pl.BlockSpec((1, tq, LANES), lambda bi, qi: (bi, qi, KA_BLK)),
            pl.BlockSpec((1, SUB, LANES), lambda bi, qi: (bi, jnp.maximum(qi * nsub - 1, 0), KA_BLK)),
            pl.BlockSpec((1, tq, LANES), lambda bi, qi: (bi, qi, VA_BLK)),
            pl.BlockSpec((1, SUB, LANES), lambda bi, qi: (bi, jnp.maximum(qi * nsub - 1, 0), VA_BLK)),
            pl.BlockSpec((1, LANES), const),
            pl.BlockSpec((1, LANES), const),
            pl.BlockSpec((LANES, LANES), const),
            pl.BlockSpec(bias_t.shape, lambda bi, qi: (0, 0, 0, 0)),
            pl.BlockSpec(sink_rows.shape, lambda bi, qi: (0, 0, 0)),
        ],
        out_specs=pl.BlockSpec((1, tq, q_w), lambda bi, qi: (bi, qi, 0)),
        out_shape=jax.ShapeDtypeStruct((b, s, q_w), BF16),
        scratch_shapes=[
            pltpu.VMEM((A_KV_HEADS * nsub, 2 * SUB, A_GROUP * SUB), F32),
            pltpu.VMEM((A_KV_HEADS * nsub, 1, A_GROUP * SUB), F32),
            pltpu.VMEM((A_KV_HEADS * nsub, 2 * SUB, A_GROUP * SUB), BF16),
        ],
        compiler_params=pltpu.CompilerParams(
            dimension_semantics=("arbitrary", "arbitrary"),
            vmem_limit_bytes=VMEM_ATTENTION),
        name="swa",
    )(qkv, qkv, qkv, qkv, qkv, gq2, gk2, e2, bias_t, sink_rows)


def _fox_kernel(q_ref, k_ref, v_ref, cp_ref, selq_ref, selk_ref, gq_ref, gk_ref, e2_ref,
                o_ref, kaug_ref, qt_ref, vt_ref, mask_ref, s_ref, smax_ref, m_ref, acc_ref):
    s_len = k_ref.shape[1]
    tq = tk = PRO_TILE
    n_tiles = s_len // tq
    e2 = e2_ref[...]
    lane = _lane_iota((1, LANES))
    low = lane < HALF
    own = (low, jnp.logical_not(low))
    one_lane = (HALF, 0)

    def prepare(i, _):
        rows = pl.ds(pl.multiple_of(i * PRO_TILE, PRO_TILE), PRO_TILE)
        kc = k_ref[0, rows, :].astype(F32)
        kn = kc * _head_rms_scale(kc, e2) * gk_ref[...]
        qc = q_ref[0, rows, :].astype(F32)
        qn = qc * _head_rms_scale(qc, e2) * gq_ref[...]
        vc = v_ref[0, rows, :]
        cp = cp_ref[0, rows, :]
        extra_k = jnp.dot(cp, selk_ref[0], preferred_element_type=F32)
        extra_q = jnp.dot(cp, selq_ref[0], preferred_element_type=F32)
        for hh in range(2):
            kaug_ref[hh, rows, :] = jnp.where(own[hh], kn, extra_k).astype(BF16)
            qt_ref[hh, i] = jnp.where(own[hh], qn, extra_q).astype(BF16).T
            vt_ref[hh, i] = jnp.where(own[hh], vc, jnp.where(lane == one_lane[hh], 1.0, 0.0).astype(BF16)).T
        return 0
    lax.fori_loop(0, n_tiles, prepare, 0, unroll=True)

    @pl.when(jnp.logical_and(pl.program_id(0) == 0, pl.program_id(1) == 0))
    def _build_mask():
        key = lax.broadcasted_iota(jnp.int32, (tk, tq), 0)
        qry = lax.broadcasted_iota(jnp.int32, (tk, tq), 1)
        mask_ref[...] = jnp.where(key <= qry, 0.0, NEG_BIG)

    m_ref[...] = jnp.full_like(m_ref, NEG_BIG)
    acc_ref[...] = jnp.zeros_like(acc_ref)

    def issue_scores(hh, qi, kb, masked):
        krows = pl.ds(pl.multiple_of(kb * tk, tk), tk)
        nxt = jnp.dot(kaug_ref[hh, krows, :], qt_ref[hh, qi], preferred_element_type=F32)
        if masked:
            nxt = nxt + mask_ref[...]
        s_ref[hh] = nxt
        smax_ref[hh] = jnp.max(nxt, axis=0, keepdims=True)

    half = tk // 2

    def issue_diagonal(hh, d):
        r0 = d * tk
        q_t = qt_ref[hh, d]
        top = jnp.dot(kaug_ref[hh, r0:r0 + half, :], q_t, preferred_element_type=F32) + mask_ref[:half, :]
        bot = (jnp.dot(kaug_ref[hh, r0 + half:r0 + tk, :], q_t[:, half:], preferred_element_type=F32)
               + mask_ref[half:, half:])
        s_ref[hh, :half, :] = top
        s_ref[hh, half:, half:] = bot
        s_ref[hh, half:, :half] = jnp.full((half, half), NEG_BIG, F32)
        top_max = jnp.max(top, axis=0, keepdims=True)
        smax_ref[hh] = jnp.concatenate(
            [top_max[:, :half], jnp.maximum(top_max[:, half:], jnp.max(bot, axis=0, keepdims=True))], axis=1)

    def consume(hh, qi1, kb1, s, smax):
        m_old = m_ref[hh, qi1]
        m_new = jnp.maximum(m_old, smax)
        p = jnp.exp2(s - m_new).astype(BF16)
        acc_ref[hh, qi1] = (jnp.exp2(m_old - m_new) * acc_ref[hh, qi1]
                            + jnp.dot(vt_ref[hh, kb1], p, preferred_element_type=F32))
        m_ref[hh, qi1] = m_new

    def consume_diagonal(hh, d, s_top, s_bot, smax):
        m_old = m_ref[hh, d]
        m_new = jnp.maximum(m_old, smax)
        p_top = jnp.exp2(s_top - m_new).astype(BF16)
        p_bot = jnp.exp2(s_bot - m_new[:, half:]).astype(BF16)
        v_t = vt_ref[hh, d]
        pv = jnp.dot(v_t[:, :half], p_top, preferred_element_type=F32)
        pv_late = jnp.dot(v_t[:, half:], p_bot, preferred_element_type=F32)
        acc = jnp.exp2(m_old - m_new) * acc_ref[hh, d] + pv
        acc_ref[hh, d] = jnp.concatenate([acc[:, :half], acc[:, half:] + pv_late], axis=1)
        m_ref[hh, d] = m_new

    n_off = n_tiles * (n_tiles - 1) // 2
    zero = jnp.int32(0)
    for hh in range(2):
        issue_diagonal(hh, 0)

    def off_diagonal(carry):
        qi0, kb0, qi1, kb1 = carry
        for hh in range(2):
            s_prev, smax_prev = s_ref[hh], smax_ref[hh]
            issue_scores(hh, qi0, kb0, False)
            consume(hh, qi1, kb1, s_prev, smax_prev)
        wrap = kb0 + 1 == qi0
        return jnp.where(wrap, qi0 + 1, qi0), jnp.where(wrap, 0, kb0 + 1), qi0, kb0

    def trip(_, carry):
        for _ in range(FOX_STEPS_PER_TRIP):
            carry = off_diagonal(carry)
        return carry
    carry = (jnp.int32(1), zero, zero, zero)
    carry = lax.fori_loop(0, n_off // FOX_STEPS_PER_TRIP, trip, carry)
    for _ in range(n_off % FOX_STEPS_PER_TRIP):
        carry = off_diagonal(carry)
    for d in range(1, n_tiles):
        for hh in range(2):
            if d == 1:
                s_prev, smax_prev = s_ref[hh], smax_ref[hh]
                issue_diagonal(hh, d)
                consume(hh, carry[2], carry[3], s_prev, smax_prev)
            else:
                s_top, s_bot, smax_prev = s_ref[hh, :half, :], s_ref[hh, half:, half:], smax_ref[hh]
                issue_diagonal(hh, d)
                consume_diagonal(hh, d - 1, s_top, s_bot, smax_prev)
    for hh in range(2):
        consume_diagonal(hh, n_tiles - 1, s_ref[hh, :half, :], s_ref[hh, half:, half:], smax_ref[hh])

    feat = lax.broadcasted_iota(jnp.int32, (LANES, 1), 0)

    def finish(i, _):
        outs = []
        for hh in range(2):
            acc = acc_ref[hh, i]
            outs.append(acc * (1.0 / acc[one_lane[hh]:one_lane[hh] + 1, :]))
        rows = pl.ds(pl.multiple_of(i * tq, tq), tq)
        o_ref[0, rows, :] = jnp.where(feat < HALF, outs[0], outs[1]).astype(o_ref.dtype).T
        return 0
    lax.fori_loop(0, n_tiles, finish, 0, unroll=True)


def _fox(qkv, cp, selq, selk, gq2, gk2, e2):
    b, s, _ = qkv.shape
    assert s % PRO_TILE == 0
    npair = B_HEADS // 2
    nblk = s // PRO_TILE
    const = lambda bi, p: (0, 0)
    return pl.pallas_call(
        _fox_kernel,
        grid=(b, npair),
        in_specs=[
            pl.BlockSpec((1, s, LANES), lambda bi, p: (bi, 0, QB_BLK + p)),
            pl.BlockSpec((1, s, LANES), lambda bi, p: (bi, 0, KB_BLK + p)),
            pl.BlockSpec((1, s, LANES), lambda bi, p: (bi, 0, VB_BLK + p)),
            pl.BlockSpec((1, s, LANES), lambda bi, p: (bi, 0, 0)),
            pl.BlockSpec((1, LANES, LANES), lambda bi, p: (p, 0, 0)),
            pl.BlockSpec((1, LANES, LANES), lambda bi, p: (p, 0, 0)),
            pl.BlockSpec((1, LANES), const),
            pl.BlockSpec((1, LANES), const),
            pl.BlockSpec((LANES, LANES), const),
        ],
        out_specs=pl.BlockSpec((1, s, LANES), lambda bi, p: (bi, 0, p)),
        out_shape=jax.ShapeDtypeStruct((b, s, B_HEADS * HEAD_DIM), BF16),
        scratch_shapes=[
            pltpu.VMEM((2, s, LANES), BF16),
            pltpu.VMEM((2, nblk, LANES, PRO_TILE), BF16),
            pltpu.VMEM((2, nblk, LANES, PRO_TILE), BF16),
            pltpu.VMEM((PRO_TILE, PRO_TILE), F32),
            pltpu.VMEM((2, PRO_TILE, PRO_TILE), F32),
            pltpu.VMEM((2, 1, PRO_TILE), F32),
            pltpu.VMEM((2, nblk, 1, PRO_TILE), F32),
            pltpu.VMEM((2, nblk, LANES, PRO_TILE), F32),
        ],
        compiler_params=pltpu.CompilerParams(
            dimension_semantics=("arbitrary", "arbitrary"),
            vmem_limit_bytes=VMEM_ATTENTION),
        name="fox",
    )(qkv, qkv, qkv, cp, selq, selk, gq2, gk2, e2)


def _out_mlp_kernel(x_ref, oa_ref, ob_ref, woa_ref, wob_ref, g_ref, wup_ref, wdn_ref, y_ref):
    h = (x_ref[...]
         + jnp.dot(oa_ref[...], woa_ref[...], preferred_element_type=F32)
         + jnp.dot(ob_ref[...], wob_ref[...], preferred_element_type=F32))
    ms = jnp.sum(h * h, axis=-1, keepdims=True) * (1.0 / D_MODEL)
    hn = (h * lax.rsqrt(ms + EPS) * g_ref[...]).astype(BF16)
    u = jnp.maximum(jnp.dot(hn, wup_ref[...], preferred_element_type=F32), 0.0)
    y_ref[...] = h + jnp.dot((u * u).astype(BF16), wdn_ref[...], preferred_element_type=F32)


def _out_mlp(x2, oa2, ob2, woa, wob, g2, wup, wdn):
    n, d = x2.shape
    tm = TOK_TILE
    const = lambda i: (0, 0)
    resident = dict(pipeline_mode=pl.Buffered(1))
    return pl.pallas_call(
        _out_mlp_kernel,
        grid=(n // tm,),
        in_specs=[
            pl.BlockSpec((tm, d), lambda i: (i, 0)),
            pl.BlockSpec((tm, oa2.shape[1]), lambda i: (i, 0)),
            pl.BlockSpec((tm, ob2.shape[1]), lambda i: (i, 0)),
            pl.BlockSpec(woa.shape, const, **resident),
            pl.BlockSpec(wob.shape, const, **resident),
            pl.BlockSpec((1, d), const),
            pl.BlockSpec(wup.shape, const, **resident),
            pl.BlockSpec(wdn.shape, const, **resident),
        ],
        out_specs=pl.BlockSpec((tm, d), lambda i: (i, 0)),
        out_shape=jax.ShapeDtypeStruct((n, d), F32),
        compiler_params=pltpu.CompilerParams(
            dimension_semantics=("arbitrary",),
            vmem_limit_bytes=VMEM_OUT_MLP),
        name="out_mlp",
    )(x2, oa2, ob2, woa, wob, g2, wup, wdn)


def _block_diag_ones():
    i = np.arange(LANES)
    return jnp.asarray((i[:, None] // HALF == i[None, :] // HALF).astype(np.float32), dtype=BF16)


def _fox_select_tables():
    selq = np.zeros((B_HEADS // 2, LANES, LANES), np.float32)
    selk = np.zeros_like(selq)
    for p in range(B_HEADS // 2):
        for hh in range(2):
            h = 2 * p + hh
            base = HALF if hh == 0 else 0
            for piece in range(3):
                selq[p, piece * PIECE_STRIDE + h, base + piece] = 1.0
                selq[p, ONE_LANE, base + 3 + piece] = 1.0
                selk[p, ONE_LANE, base + piece] = 1.0
                selk[p, piece * PIECE_STRIDE + h, base + 3 + piece] = -1.0
    return jnp.asarray(selq, dtype=BF16), jnp.asarray(selk, dtype=BF16)


def _swa_bias_tables():
    slopes = np.exp2(-(8.0 / A_HEADS) * (np.arange(A_HEADS, dtype=np.float32) + 1.0)).astype(np.float32)
    qpos = SUB + np.arange(SUB)
    kpos = np.arange(2 * SUB)
    dist = kpos[:, None] * 0 + qpos[None, :] - kpos[:, None]
    band = (dist >= 0) & (dist < WINDOW)
    out = np.zeros((A_KV_HEADS, 2, 2 * SUB, A_GROUP * SUB), np.float32)
    for g in range(A_KV_HEADS):
        for j in range(A_GROUP):
            alibi = (-slopes[g * A_GROUP + j] * dist.astype(np.float32)).astype(np.float32) * np.float32(LOG2E)
            cols = slice(j * SUB, (j + 1) * SUB)
            out[g, 0, :, cols] = np.where(band, alibi, NEG_BIG)
            out[g, 1, :, cols] = np.where(band & (kpos >= SUB)[:, None], alibi, NEG_BIG)
    return jnp.asarray(out)


def kernel(x, attn_norm_g, w_in, b_forget, q_norm_a, k_norm_a, sink_logits, q_norm_b, k_norm_b, w_out,
           mlp_norm_g, w_up, w_down):
    b, s, d = x.shape
    assert d == D_MODEL and s % Q_TILE == 0 and s % PROJ_TILE == 0 and (b * s) % TOK_TILE == 0
    scale = 1.0 / math.sqrt(HEAD_DIM)

    w_pad = jnp.pad(w_in, ((0, 0), (0, PROJ_W - w_in.shape[1]))).astype(BF16)
    bf_pad = jnp.pad(b_forget.astype(F32), (0, LANES - B_HEADS)).reshape(1, LANES)
    e2 = _block_diag_ones()
    selq, selk = _fox_select_tables()
    bias = _swa_bias_tables()
    sink_rows = jnp.repeat(sink_logits.astype(F32).reshape(A_KV_HEADS, 1, A_GROUP) * LOG2E, SUB, axis=2)
    root = math.sqrt(HEAD_DIM)
    gqa2 = (jnp.tile(q_norm_a.astype(F32), 2) * (root * scale * LOG2E)).reshape(1, LANES)
    gka2 = (jnp.tile(k_norm_a.astype(F32), 2) * root).reshape(1, LANES)
    gqb2 = (jnp.tile(q_norm_b.astype(F32), 2) * (root * scale * LOG2E)).reshape(1, LANES)
    gkb2 = (jnp.tile(k_norm_b.astype(F32), 2) * root).reshape(1, LANES)

    qkv, cp = _in_proj(x, attn_norm_g.astype(F32).reshape(1, d), w_pad, bf_pad)
    out_a = _swa(qkv, gqa2, gka2, e2, bias, sink_rows)
    out_b = _fox(qkv, cp, selq, selk, gqb2, gkb2, e2)

    wo = w_out.astype(BF16)
    y = _out_mlp(x.reshape(b * s, d), out_a.reshape(b * s, -1), out_b.reshape(b * s, -1),
                 wo[:A_HEADS * HEAD_DIM], wo[A_HEADS * HEAD_DIM:],
                 mlp_norm_g.astype(F32).reshape(1, d), w_up.astype(BF16), w_down.astype(BF16))
    return y.reshape(b, s, d)
```

```python
import math

import numpy as np
import jax
import jax.numpy as jnp
from jax import lax
from jax.experimental import pallas as pl
from jax.experimental.pallas import tpu as pltpu

F32 = jnp.float32
BF16 = jnp.bfloat16

D_MODEL = 1024
HEAD_DIM = 64
A_HEADS = 8
A_KV_HEADS = 2
A_GROUP = A_HEADS // A_KV_HEADS
B_HEADS = 8
WINDOW = 128
EPS = 1e-6
LOG2E = 1.4426950408889634
NEG_BIG = -1e30

LANES = 128
HALF = LANES // 2

QA_BLK = 0
KA_BLK = 4
VA_BLK = 5
QB_BLK = 6
KB_BLK = 10
VB_BLK = 14
QKV_W = 18 * LANES
PROJ_W = QKV_W + LANES

PIECE_STRIDE = 8
ONE_LANE = 3 * PIECE_STRIDE

PROJ_TILE = 1024
PROJ_PARTS = 2
TOK_TILE = 512
Q_TILE = 2048
PRO_TILE = 512
FOX_STEPS_PER_TRIP = 14
SUB = 128

MIB = 1024 * 1024
VMEM_IN_PROJ = 48 * MIB
VMEM_ATTENTION = 48 * MIB
VMEM_OUT_MLP = 52 * MIB


def _lane_iota(shape):
    return lax.broadcasted_iota(jnp.int32, shape, len(shape) - 1)


def _split3(v):
    hi = v.astype(BF16).astype(F32)
    r = v - hi
    mid = r.astype(BF16).astype(F32)
    lo = (r - mid).astype(BF16).astype(F32)
    return hi, mid, lo


def _head_rms_scale(t, e2):
    sq = t * t
    hi = sq.astype(BF16)
    lo = (sq - hi.astype(F32)).astype(BF16)
    ss = (jnp.dot(hi, e2, preferred_element_type=F32) + jnp.dot(lo, e2, preferred_element_type=F32))
    return lax.rsqrt(ss + HEAD_DIM * EPS)


def _in_proj_kernel(x_ref, g_ref, w_ref, bf_ref, qkv_ref, cp_ref, carry_ref):
    @pl.when(pl.program_id(1) == 0)
    def _():
        carry_ref[...] = jnp.zeros_like(carry_ref)

    tm = x_ref.shape[1]
    n_parts = PROJ_PARTS
    parts = [slice(h * (tm // n_parts), (h + 1) * (tm // n_parts)) for h in range(n_parts)]
    xn = []
    for rows in parts:
        x = x_ref[0, rows, :]
        ms = jnp.sum(x * x, axis=-1, keepdims=True) * (1.0 / D_MODEL)
        xn.append((x * lax.rsqrt(ms + EPS) * g_ref[...]).astype(BF16))
    for rows, h in zip(parts[:-1], xn[:-1]):
        qkv_ref[0, rows, :] = jnp.dot(h, w_ref[:, :QKV_W], preferred_element_type=F32).astype(BF16)
    z = jnp.concatenate([jnp.dot(h, w_ref[:, QKV_W:], preferred_element_type=F32) for h in xn], axis=0)
    qkv_ref[0, parts[-1], :] = jnp.dot(xn[-1], w_ref[:, :QKV_W], preferred_element_type=F32).astype(BF16)

    z = z + bf_ref[...]
    lane = _lane_iota(z.shape)
    logf = jnp.minimum(z, 0.0) - jnp.log(1.0 + jnp.exp(-jnp.abs(z)))
    logf = jnp.where(lane < B_HEADS, logf * LOG2E, 0.0)

    row = lax.broadcasted_iota(jnp.int32, z.shape, 0)
    c = logf
    shift = 1
    while shift < tm:
        c = c + jnp.where(row >= shift, pltpu.roll(c, shift, 0), 0.0)
        shift *= 2
    c = c + carry_ref[...]
    carry_ref[...] = c[tm - 1:, :]

    chi, cmid, clo = _split3(c)
    pieces = (chi + pltpu.roll(cmid, PIECE_STRIDE, 1) + pltpu.roll(clo, 2 * PIECE_STRIDE, 1)
              + jnp.where(lane == ONE_LANE, 1.0, 0.0))
    cp_ref[0] = pieces.astype(BF16)


def _in_proj(x, g, w_pad, bf_pad):
    b, s, d = x.shape
    tm = PROJ_TILE
    const = lambda bi, si: (0, 0)
    return pl.pallas_call(
        _in_proj_kernel,
        grid=(b, s // tm),
        in_specs=[
            pl.BlockSpec((1, tm, d), lambda bi, si: (bi, si, 0)),
            pl.BlockSpec((1, d), const),
            pl.BlockSpec((d, PROJ_W), const),
            pl.BlockSpec((1, LANES), const),
        ],
        out_specs=[
            pl.BlockSpec((1, tm, QKV_W), lambda bi, si: (bi, si, 0)),
            pl.BlockSpec((1, tm, LANES), lambda bi, si: (bi, si, 0)),
        ],
        out_shape=[
            jax.ShapeDtypeStruct((b, s, QKV_W), BF16),
            jax.ShapeDtypeStruct((b, s, LANES), BF16),
        ],
        scratch_shapes=[pltpu.VMEM((1, LANES), F32)],
        compiler_params=pltpu.CompilerParams(
            dimension_semantics=("arbitrary", "arbitrary"),
            vmem_limit_bytes=VMEM_IN_PROJ),
        name="in_proj",
    )(x, g, w_pad, bf_pad)


def _swa_kernel(q_ref, kc_ref, kp_ref, vc_ref, vp_ref, gq_ref, gk_ref, e2_ref, bias_ref, sink_ref, o_ref,
                s_ref, smax_ref, p_ref):
    qi = pl.program_id(1)
    tq = q_ref.shape[1]
    e2 = e2_ref[...]
    variant = jnp.where(qi == 0, 1, 0)
    feat = lax.broadcasted_iota(jnp.int32, (LANES, 1), 0)

    k2 = jnp.concatenate([kp_ref[0], kc_ref[0]], axis=0).astype(F32)
    kn = (k2 * _head_rms_scale(k2, e2) * gk_ref[...]).astype(BF16)
    v_t = jnp.concatenate([vp_ref[0], vc_ref[0]], axis=0).T
    one_row = (HALF, 0)
    v_ts = [jnp.where(feat == one_row[g], jnp.ones((), BF16), v_t) for g in range(A_KV_HEADS)]

    zeros = jnp.zeros((HALF, tq), BF16)
    q_ts = []
    for c in range(A_HEADS // 2):
        g = c // (A_GROUP // 2)
        qc = q_ref[0, :, c * LANES:(c + 1) * LANES].astype(F32)
        qn_t = (qc * _head_rms_scale(qc, e2) * gq_ref[...]).astype(BF16).T
        for half in range(2):
            f = qn_t[half * HALF:(half + 1) * HALF]
            q_ts.append(jnp.concatenate([f, zeros] if g == 0 else [zeros, f], axis=0))

    n_sub = tq // SUB
    chains = [(g, r) for g in range(A_KV_HEADS) for r in range(n_sub)]
    for c, (g, r) in enumerate(chains):
        q_t = jnp.concatenate([q_ts[A_GROUP * g + j][:, r * SUB:(r + 1) * SUB] for j in range(A_GROUP)], axis=1)
        s = jnp.dot(kn[r * SUB:r * SUB + 2 * SUB], q_t, preferred_element_type=F32)
        s = s + (bias_ref[g, variant] if r == 0 else bias_ref[g, 0])
        s_ref[c] = s
        smax_ref[c] = jnp.max(s, axis=0, keepdims=True)
    for c, (g, r) in enumerate(chains):
        m = jnp.maximum(smax_ref[c], sink_ref[g])
        p_ref[c] = jnp.exp2(s_ref[c] - m).astype(BF16)
    out_heads = [[] for _ in range(A_HEADS)]
    for c, (g, r) in enumerate(chains):
        sink = sink_ref[g]
        m = jnp.maximum(smax_ref[c], sink)
        pv = jnp.dot(v_ts[g][:, r * SUB:r * SUB + 2 * SUB], p_ref[c], preferred_element_type=F32)
        denom = pv[one_row[g]:one_row[g] + 1, :] + jnp.exp2(sink - m)
        o = pv[g * HALF:(g + 1) * HALF] * (1.0 / denom)
        for j in range(A_GROUP):
            out_heads[A_GROUP * g + j].append(o[:, j * SUB:(j + 1) * SUB])
    for c in range(A_HEADS // 2):
        pair_t = jnp.concatenate([jnp.concatenate(out_heads[2 * c], axis=1),
                                  jnp.concatenate(out_heads[2 * c + 1], axis=1)], axis=0)
        o_ref[0, :, c * LANES:(c + 1) * LANES] = pair_t.astype(o_ref.dtype).T


def _swa(qkv, gq2, gk2, e2, bias_t, sink_rows):
    b, s, _ = qkv.shape
    tq = Q_TILE
    nsub = tq // SUB
    q_w = A_HEADS * HEAD_DIM
    const = lambda bi, qi: (0, 0)
    return pl.pallas_call(
        _swa_kernel,
        grid=(b, s // tq),
        in_specs=[
            pl.BlockSpec((1, tq, q_w), lambda bi, qi: (bi, qi, 0)),
            pl.BlockSpec((1, tq, LANES), lambda bi, qi: (bi, qi, KA_BLK)),
            pl.BlockSpec((1, SUB, LANES), lambda bi, qi: (bi, jnp.maximum(qi * nsub - 1, 0), KA_BLK)),
            pl.BlockSpec((1, tq, LANES), lambda bi, qi: (bi, qi, VA_BLK)),
            pl.BlockSpec((1, SUB, LANES), lambda bi, qi: (bi, jnp.maximum(qi * nsub - 1, 0), VA_BLK)),
            pl.BlockSpec((1, LANES), const),
            pl.BlockSpec((1, LANES), const),
            pl.BlockSpec((LANES, LANES), const),
            pl.BlockSpec(bias_t.shape, lambda bi, qi: (0, 0, 0, 0)),
            pl.BlockSpec(sink_rows.shape, lambda bi, qi: (0, 0, 0)),
        ],
        out_specs=pl.BlockSpec((1, tq, q_w), lambda bi, qi: (bi, qi, 0)),
        out_shape=jax.ShapeDtypeStruct((b, s, q_w), BF16),
        scratch_shapes=[
            pltpu.VMEM((A_KV_HEADS * nsub, 2 * SUB, A_GROUP * SUB), F32),
            pltpu.VMEM((A_KV_HEADS * nsub, 1, A_GROUP * SUB), F32),
            pltpu.VMEM((A_KV_HEADS * nsub, 2 * SUB, A_GROUP * SUB), BF16),
        ],
        compiler_params=pltpu.CompilerParams(
            dimension_semantics=("arbitrary", "arbitrary"),
            vmem_limit_bytes=VMEM_ATTENTION),
        name="swa",
    )(qkv, qkv, qkv, qkv, qkv, gq2, gk2, e2, bias_t, sink_rows)


def _fox_kernel(q_ref, k_ref, v_ref, cp_ref, selq_ref, selk_ref, gq_ref, gk_ref, e2_ref,
                o_ref, kaug_ref, qt_ref, vt_ref, mask_ref, s_ref, smax_ref, m_ref, acc_ref):
    s_len = k_ref.shape[1]
    tq = tk = PRO_TILE
    n_tiles = s_len // tq
    e2 = e2_ref[...]
    lane = _lane_iota((1, LANES))
    low = lane < HALF
    own = (low, jnp.logical_not(low))
    one_lane = (HALF, 0)

    def prepare(i, _):
        rows = pl.ds(pl.multiple_of(i * PRO_TILE, PRO_TILE), PRO_TILE)
        kc = k_ref[0, rows, :].astype(F32)
        kn = kc * _head_rms_scale(kc, e2) * gk_ref[...]
        qc = q_ref[0, rows, :].astype(F32)
        qn = qc * _head_rms_scale(qc, e2) * gq_ref[...]
        vc = v_ref[0, rows, :]
        cp = cp_ref[0, rows, :]
        extra_k = jnp.dot(cp, selk_ref[0], preferred_element_type=F32)
        extra_q = jnp.dot(cp, selq_ref[0], preferred_element_type=F32)
        for hh in range(2):
            kaug_ref[hh, rows, :] = jnp.where(own[hh], kn, extra_k).astype(BF16)
            qt_ref[hh, i] = jnp.where(own[hh], qn, extra_q).astype(BF16).T
            vt_ref[hh, i] = jnp.where(own[hh], vc, jnp.where(lane == one_lane[hh], 1.0, 0.0).astype(BF16)).T
        return 0
    lax.fori_loop(0, n_tiles, prepare, 0, unroll=True)

    @pl.when(jnp.logical_and(pl.program_id(0) == 0, pl.program_id(1) == 0))
    def _build_mask():
        key = lax.broadcasted_iota(jnp.int32, (tk, tq), 0)
        qry = lax.broadcasted_iota(jnp.int32, (tk, tq), 1)
        mask_ref[...] = jnp.where(key <= qry, 0.0, NEG_BIG)

    m_ref[...] = jnp.full_like(m_ref, NEG_BIG)
    acc_ref[...] = jnp.zeros_like(acc_ref)

    def issue_scores(hh, qi, kb, masked):
        krows = pl.ds(pl.multiple_of(kb * tk, tk), tk)
        nxt = jnp.dot(kaug_ref[hh, krows, :], qt_ref[hh, qi], preferred_element_type=F32)
        if masked:
            nxt = nxt + mask_ref[...]
        s_ref[hh] = nxt
        smax_ref[hh] = jnp.max(nxt, axis=0, keepdims=True)

    half = tk // 2

    def issue_diagonal(hh, d):
        r0 = d * tk
        q_t = qt_ref[hh, d]
        top = jnp.dot(kaug_ref[hh, r0:r0 + half, :], q_t, preferred_element_type=F32) + mask_ref[:half, :]
        bot = (jnp.dot(kaug_ref[hh, r0 + half:r0 + tk, :], q_t[:, half:], preferred_element_type=F32)
               + mask_ref[half:, half:])
        s_ref[hh, :half, :] = top
        s_ref[hh, half:, half:] = bot
        s_ref[hh, half:, :half] = jnp.full((half, half), NEG_BIG, F32)
        top_max = jnp.max(top, axis=0, keepdims=True)
        smax_ref[hh] = jnp.concatenate(
            [top_max[:, :half], jnp.maximum(top_max[:, half:], jnp.max(bot, axis=0, keepdims=True))], axis=1)

    def consume(hh, qi1, kb1, s, smax):
        m_old = m_ref[hh, qi1]
        m_new = jnp.maximum(m_old, smax)
        p = jnp.exp2(s - m_new).astype(BF16)
        acc_ref[hh, qi1] = (jnp.exp2(m_old - m_new) * acc_ref[hh, qi1]
                            + jnp.dot(vt_ref[hh, kb1], p, preferred_element_type=F32))
        m_ref[hh, qi1] = m_new

    def consume_diagonal(hh, d, s_top, s_bot, smax):
        m_old = m_ref[hh, d]
        m_new = jnp.maximum(m_old, smax)
        p_top = jnp.exp2(s_top - m_new).astype(BF16)
        p_bot = jnp.exp2(s_bot - m_new[:, half:]).astype(BF16)
        v_t = vt_ref[hh, d]
        pv = jnp.dot(v_t[:, :half], p_top, preferred_element_type=F32)
        pv_late = jnp.dot(v_t[:, half:], p_bot, preferred_element_type=F32)
        acc = jnp.exp2(m_old - m_new) * acc_ref[hh, d] + pv
        acc_ref[hh, d] = jnp.concatenate([acc[:, :half], acc[:, half:] + pv_late], axis=1)
        m_ref[hh, d] = m_new

    n_off = n_tiles * (n_tiles - 1) // 2
    zero = jnp.int32(0)
    for hh in range(2):
        issue_diagonal(hh, 0)

    def off_diagonal(carry):
        qi0, kb0, qi1, kb1 = carry
        for hh in range(2):
            s_prev, smax_prev = s_ref[hh], smax_ref[hh]
            issue_scores(hh, qi0, kb0, False)
            consume(hh, qi1, kb1, s_prev, smax_prev)
        wrap = kb0 + 1 == qi0
        return jnp.where(wrap, qi0 + 1, qi0), jnp.where(wrap, 0, kb0 + 1), qi0, kb0

    def trip(_, carry):
        for _ in range(FOX_STEPS_PER_TRIP):
            carry = off_diagonal(carry)
        return carry
    carry = (jnp.int32(1), zero, zero, zero)
    carry = lax.fori_loop(0, n_off // FOX_STEPS_PER_TRIP, trip, carry)
    for _ in range(n_off % FOX_STEPS_PER_TRIP):
        carry = off_diagonal(carry)
    for d in range(1, n_tiles):
        for hh in range(2):
            if d == 1:
                s_prev, smax_prev = s_ref[hh], smax_ref[hh]
                issue_diagonal(hh, d)
                consume(hh, carry[2], carry[3], s_prev, smax_prev)
            else:
                s_top, s_bot, smax_prev = s_ref[hh, :half, :], s_ref[hh, half:, half:], smax_ref[hh]
                issue_diagonal(hh, d)
                consume_diagonal(hh, d - 1, s_top, s_bot, smax_prev)
    for hh in range(2):
        consume_diagonal(hh, n_tiles - 1, s_ref[hh, :half, :], s_ref[hh, half:, half:], smax_ref[hh])

    feat = lax.broadcasted_iota(jnp.int32, (LANES, 1), 0)

    def finish(i, _):
        outs = []
        for hh in range(2):
            acc = acc_ref[hh, i]
            outs.append(acc * (1.0 / acc[one_lane[hh]:one_lane[hh] + 1, :]))
        rows = pl.ds(pl.multiple_of(i * tq, tq), tq)
        o_ref[0, rows, :] = jnp.where(feat < HALF, outs[0], outs[1]).astype(o_ref.dtype).T
        return 0
    lax.fori_loop(0, n_tiles, finish, 0, unroll=True)


def _fox(qkv, cp, selq, selk, gq2, gk2, e2):
    b, s, _ = qkv.shape
    assert s % PRO_TILE == 0
    npair = B_HEADS // 2
    nblk = s // PRO_TILE
    const = lambda bi, p: (0, 0)
    return pl.pallas_call(
        _fox_kernel,
        grid=(b, npair),
        in_specs=[
            pl.BlockSpec((1, s, LANES), lambda bi, p: (bi, 0, QB_BLK + p)),
            pl.BlockSpec((1, s, LANES), lambda bi, p: (bi, 0, KB_BLK + p)),
            pl.BlockSpec((1, s, LANES), lambda bi, p: (bi, 0, VB_BLK + p)),
            pl.BlockSpec((1, s, LANES), lambda bi, p: (bi, 0, 0)),
            pl.BlockSpec((1, LANES, LANES), lambda bi, p: (p, 0, 0)),
            pl.BlockSpec((1, LANES, LANES), lambda bi, p: (p, 0, 0)),
            pl.BlockSpec((1, LANES), const),
            pl.BlockSpec((1, LANES), const),
            pl.BlockSpec((LANES, LANES), const),
        ],
        out_specs=pl.BlockSpec((1, s, LANES), lambda bi, p: (bi, 0, p)),
        out_shape=jax.ShapeDtypeStruct((b, s, B_HEADS * HEAD_DIM), BF16),
        scratch_shapes=[
            pltpu.VMEM((2, s, LANES), BF16),
            pltpu.VMEM((2, nblk, LANES, PRO_TILE), BF16),
            pltpu.VMEM((2, nblk, LANES, PRO_TILE), BF16),
            pltpu.VMEM((PRO_TILE, PRO_TILE), F32),
            pltpu.VMEM((2, PRO_TILE, PRO_TILE), F32),
            pltpu.VMEM((2, 1, PRO_TILE), F32),
            pltpu.VMEM((2, nblk, 1, PRO_TILE), F32),
            pltpu.VMEM((2, nblk, LANES, PRO_TILE), F32),
        ],
        compiler_params=pltpu.CompilerParams(
            dimension_semantics=("arbitrary", "arbitrary"),
            vmem_limit_bytes=VMEM_ATTENTION),
        name="fox",
    )(qkv, qkv, qkv, cp, selq, selk, gq2, gk2, e2)


def _out_mlp_kernel(x_ref, oa_ref, ob_ref, woa_ref, wob_ref, g_ref, wup_ref, wdn_ref, y_ref):
    h = (x_ref[...]
         + jnp.dot(oa_ref[...], woa_ref[...], preferred_element_type=F32)
         + jnp.dot(ob_ref[...], wob_ref[...], preferred_element_type=F32))
    ms = jnp.sum(h * h, axis=-1, keepdims=True) * (1.0 / D_MODEL)
    hn = (h * lax.rsqrt(ms + EPS) * g_ref[...]).astype(BF16)
    u = jnp.maximum(jnp.dot(hn, wup_ref[...], preferred_element_type=F32), 0.0)
    y_ref[...] = h + jnp.dot((u * u).astype(BF16), wdn_ref[...], preferred_element_type=F32)


def _out_mlp(x2, oa2, ob2, woa, wob, g2, wup, wdn):
    n, d = x2.shape
    tm = TOK_TILE
    const = lambda i: (0, 0)
    resident = dict(pipeline_mode=pl.Buffered(1))
    return pl.pallas_call(
        _out_mlp_kernel,
        grid=(n // tm,),
        in_specs=[
            pl.BlockSpec((tm, d), lambda i: (i, 0)),
            pl.BlockSpec((tm, oa2.shape[1]), lambda i: (i, 0)),
            pl.BlockSpec((tm, ob2.shape[1]), lambda i: (i, 0)),
            pl.BlockSpec(woa.shape, const, **resident),
            pl.BlockSpec(wob.shape, const, **resident),
            pl.BlockSpec((1, d), const),
            pl.BlockSpec(wup.shape, const, **resident),
            pl.BlockSpec(wdn.shape, const, **resident),
        ],
        out_specs=pl.BlockSpec((tm, d), lambda i: (i, 0)),
        out_shape=jax.ShapeDtypeStruct((n, d), F32),
        compiler_params=pltpu.CompilerParams(
            dimension_semantics=("arbitrary",),
            vmem_limit_bytes=VMEM_OUT_MLP),
        name="out_mlp",
    )(x2, oa2, ob2, woa, wob, g2, wup, wdn)


def _block_diag_ones():
    i = np.arange(LANES)
    return jnp.asarray((i[:, None] // HALF == i[None, :] // HALF).astype(np.float32), dtype=BF16)


def _fox_select_tables():
    selq = np.zeros((B_HEADS // 2, LANES, LANES), np.float32)
    selk = np.zeros_like(selq)
    for p in range(B_HEADS // 2):
        for hh in range(2):
            h = 2 * p + hh
            base = HALF if hh == 0 else 0
            for piece in range(3):
                selq[p, piece * PIECE_STRIDE + h, base + piece] = 1.0
                selq[p, ONE_LANE, base + 3 + piece] = 1.0
                selk[p, ONE_LANE, base + piece] = 1.0
                selk[p, piece * PIECE_STRIDE + h, base + 3 + piece] = -1.0
    return jnp.asarray(selq, dtype=BF16), jnp.asarray(selk, dtype=BF16)


def _swa_bias_tables():
    slopes = np.exp2(-(8.0 / A_HEADS) * (np.arange(A_HEADS, dtype=np.float32) + 1.0)).astype(np.float32)
    qpos = SUB + np.arange(SUB)
    kpos = np.arange(2 * SUB)
    dist = kpos[:, None] * 0 + qpos[None, :] - kpos[:, None]
    band = (dist >= 0) & (dist < WINDOW)
    out = np.zeros((A_KV_HEADS, 2, 2 * SUB, A_GROUP * SUB), np.float32)
    for g in range(A_KV_HEADS):
        for j in range(A_GROUP):
            alibi = (-slopes[g * A_GROUP + j] * dist.astype(np.float32)).astype(np.float32) * np.float32(LOG2E)
            cols = slice(j * SUB, (j + 1) * SUB)
            out[g, 0, :, cols] = np.where(band, alibi, NEG_BIG)
            out[g, 1, :, cols] = np.where(band & (kpos >= SUB)[:, None], alibi, NEG_BIG)
    return jnp.asarray(out)


def kernel(x, attn_norm_g, w_in, b_forget, q_norm_a, k_norm_a, sink_logits, q_norm_b, k_norm_b, w_out,
           mlp_norm_g, w_up, w_down):
    b, s, d = x.shape
    assert d == D_MODEL and s % Q_TILE == 0 and s % PROJ_TILE == 0 and (b * s) % TOK_TILE == 0
    scale = 1.0 / math.sqrt(HEAD_DIM)

    w_pad = jnp.pad(w_in, ((0, 0), (0, PROJ_W - w_in.shape[1]))).astype(BF16)
    bf_pad = jnp.pad(b_forget.astype(F32), (0, LANES - B_HEADS)).reshape(1, LANES)
    e2 = _block_diag_ones()
    selq, selk = _fox_select_tables()
    bias = _swa_bias_tables()
    sink_rows = jnp.repeat(sink_logits.astype(F32).reshape(A_KV_HEADS, 1, A_GROUP) * LOG2E, SUB, axis=2)
    root = math.sqrt(HEAD_DIM)
    gqa2 = (jnp.tile(q_norm_a.astype(F32), 2) * (root * scale * LOG2E)).reshape(1, LANES)
    gka2 = (jnp.tile(k_norm_a.astype(F32), 2) * root).reshape(1, LANES)
    gqb2 = (jnp.tile(q_norm_b.astype(F32), 2) * (root * scale * LOG2E)).reshape(1, LANES)
    gkb2 = (jnp.tile(k_norm_b.astype(F32), 2) * root).reshape(1, LANES)

    qkv, cp = _in_proj(x, attn_norm_g.astype(F32).reshape(1, d), w_pad, bf_pad)
    out_a = _swa(qkv, gqa2, gka2, e2, bias, sink_rows)
    out_b = _fox(qkv, cp, selq, selk, gqb2, gkb2, e2)

    wo = w_out.astype(BF16)
    y = _out_mlp(x.reshape(b * s, d), out_a.reshape(b * s, -1), out_b.reshape(b * s, -1),
                 wo[:A_HEADS * HEAD_DIM], wo[A_HEADS * HEAD_DIM:],
                 mlp_norm_g.astype(F32).reshape(1, d), w_up.astype(BF16), w_down.astype(BF16))
    return y.reshape(b, s, d)
```

```python
import math

import numpy as np
import jax
import jax.numpy as jnp
from jax import lax
from jax.experimental import pallas as pl
from jax.experimental.pallas import tpu as pltpu

F32 = jnp.float32
BF16 = jnp.bfloat16

D_MODEL = 1024
HEAD_DIM = 64
A_HEADS = 8
A_KV_HEADS = 2
A_GROUP = A_HEADS // A_KV_HEADS
B_HEADS = 8
WINDOW = 128
EPS = 1e-6
LOG2E = 1.4426950408889634
NEG_BIG = -1e30

LANES = 128
HALF = LANES // 2

QA_BLK = 0
KA_BLK = 4
VA_BLK = 5
QB_BLK = 6
KB_BLK = 10
VB_BLK = 14
QKV_W = 18 * LANES
PROJ_W = QKV_W + LANES

PIECE_STRIDE = 8
ONE_LANE = 3 * PIECE_STRIDE

PROJ_TILE = 1024
PROJ_PARTS = 2
TOK_TILE = 512
Q_TILE = 2048
PRO_TILE = 512
FOX_STEPS_PER_TRIP = 14
SUB = 128

MIB = 1024 * 1024
VMEM_IN_PROJ = 48 * MIB
VMEM_ATTENTION = 48 * MIB
VMEM_OUT_MLP = 52 * MIB


def _lane_iota(shape):
    return lax.broadcasted_iota(jnp.int32, shape, len(shape) - 1)


def _split3(v):
    hi = v.astype(BF16).astype(F32)
    r = v - hi
    mid = r.astype(BF16).astype(F32)
    lo = (r - mid).astype(BF16).astype(F32)
    return hi, mid, lo


def _head_rms_scale(t, e2):
    sq = t * t
    hi = sq.astype(BF16)
    lo = (sq - hi.astype(F32)).astype(BF16)
    ss = (jnp.dot(hi, e2, preferred_element_type=F32) + jnp.dot(lo, e2, preferred_element_type=F32))
    return lax.rsqrt(ss + HEAD_DIM * EPS)


def _in_proj_kernel(x_ref, g_ref, w_ref, bf_ref, gka_ref, gkb_ref, qkv_ref, cp_ref, carry_ref):
    @pl.when(pl.program_id(1) == 0)
    def _():
        carry_ref[...] = jnp.zeros_like(carry_ref)

    low = _lane_iota((1, LANES)) < HALF
    key_gain = {KA_BLK: gka_ref, **{KB_BLK + i: gkb_ref for i in range(B_HEADS // 2)}}

    def store_slab(rows, proj):
        start = 0
        for blk in sorted(key_gain):
            if blk * LANES > start:
                qkv_ref[0, rows, start:blk * LANES] = proj[:, start:blk * LANES].astype(BF16)
            t = proj[:, blk * LANES:(blk + 1) * LANES]
            sq = t * t
            s_all = jnp.sum(sq, axis=-1, keepdims=True)
            s_low = jnp.sum(jnp.where(low, sq, 0.0), axis=-1, keepdims=True)
            ss = jnp.where(low, s_low, s_all - s_low)
            kn = t * lax.rsqrt(ss + HEAD_DIM * EPS) * key_gain[blk][...]
            qkv_ref[0, rows, blk * LANES:(blk + 1) * LANES] = kn.astype(BF16)
            start = (blk + 1) * LANES
        qkv_ref[0, rows, start:] = proj[:, start:].astype(BF16)

    tm = x_ref.shape[1]
    n_parts = PROJ_PARTS
    parts = [slice(h * (tm // n_parts), (h + 1) * (tm // n_parts)) for h in range(n_parts)]
    xn = []
    for rows in parts:
        x = x_ref[0, rows, :]
        ms = jnp.sum(x * x, axis=-1, keepdims=True) * (1.0 / D_MODEL)
        xn.append((x * lax.rsqrt(ms + EPS) * g_ref[...]).astype(BF16))
    for rows, h in zip(parts[:-1], xn[:-1]):
        store_slab(rows, jnp.dot(h, w_ref[:, :QKV_W], preferred_element_type=F32))
    z = jnp.concatenate([jnp.dot(h, w_ref[:, QKV_W:], preferred_element_type=F32) for h in xn], axis=0)
    store_slab(parts[-1], jnp.dot(xn[-1], w_ref[:, :QKV_W], preferred_element_type=F32))

    z = z + bf_ref[...]
    lane = _lane_iota(z.shape)
    logf = jnp.minimum(z, 0.0) - jnp.log(1.0 + jnp.exp(-jnp.abs(z)))
    logf = jnp.where(lane < B_HEADS, logf * LOG2E, 0.0)

    row = lax.broadcasted_iota(jnp.int32, z.shape, 0)
    c = logf
    shift = 1
    while shift < tm:
        c = c + jnp.where(row >= shift, pltpu.roll(c, shift, 0), 0.0)
        shift *= 2
    c = c + carry_ref[...]
    carry_ref[...] = c[tm - 1:, :]

    chi, cmid, clo = _split3(c)
    pieces = (chi + pltpu.roll(cmid, PIECE_STRIDE, 1) + pltpu.roll(clo, 2 * PIECE_STRIDE, 1)
              + jnp.where(lane == ONE_LANE, 1.0, 0.0))
    cp_ref[0] = pieces.astype(BF16)


def _in_proj(x, g, w_pad, bf_pad, gka2, gkb2):
    b, s, d = x.shape
    tm = PROJ_TILE
    const = lambda bi, si: (0, 0)
    return pl.pallas_call(
        _in_proj_kernel,
        grid=(b, s // tm),
        in_specs=[
            pl.BlockSpec((1, tm, d), lambda bi, si: (bi, si, 0)),
            pl.BlockSpec((1, d), const),
            pl.BlockSpec((d, PROJ_W), const),
            pl.BlockSpec((1, LANES), const),
            pl.BlockSpec((1, LANES), const),
            pl.BlockSpec((1, LANES), const),
        ],
        out_specs=[
            pl.BlockSpec((1, tm, QKV_W), lambda bi, si: (bi, si, 0)),
            pl.BlockSpec((1, tm, LANES), lambda bi, si: (bi, si, 0)),
        ],
        out_shape=[
            jax.ShapeDtypeStruct((b, s, QKV_W), BF16),
            jax.ShapeDtypeStruct((b, s, LANES), BF16),
        ],
        scratch_shapes=[pltpu.VMEM((1, LANES), F32)],
        compiler_params=pltpu.CompilerParams(
            dimension_semantics=("arbitrary", "arbitrary"),
            vmem_limit_bytes=VMEM_IN_PROJ),
        name="in_proj",
    )(x, g, w_pad, bf_pad, gka2, gkb2)


def _swa_kernel(q_ref, kc_ref, kp_ref, vc_ref, vp_ref, gq_ref, e2_ref, bias_ref, sink_ref, o_ref,
                s_ref, smax_ref, p_ref):
    qi = pl.program_id(1)
    tq = q_ref.shape[1]
    e2 = e2_ref[...]
    variant = jnp.where(qi == 0, 1, 0)
    feat = lax.broadcasted_iota(jnp.int32, (LANES, 1), 0)

    kn = jnp.concatenate([kp_ref[0], kc_ref[0]], axis=0)
    v_t = jnp.concatenate([vp_ref[0], vc_ref[0]], axis=0).T
    one_row = (HALF, 0)
    v_ts = [jnp.where(feat == one_row[g], jnp.ones((), BF16), v_t) for g in range(A_KV_HEADS)]

    zeros = jnp.zeros((HALF, tq), BF16)
    q_ts = []
    for c in range(A_HEADS // 2):
        g = c // (A_GROUP // 2)
        qc = q_ref[0, :, c * LANES:(c + 1) * LANES].astype(F32)
        qn_t = (qc * _head_rms_scale(qc, e2) * gq_ref[...]).astype(BF16).T
        for half in range(2):
            f = qn_t[half * HALF:(half + 1) * HALF]
            q_ts.append(jnp.concatenate([f, zeros] if g == 0 else [zeros, f], axis=0))

    n_sub = tq // SUB
    chains = [(g, r) for g in range(A_KV_HEADS) for r in range(n_sub)]
    for c, (g, r) in enumerate(chains):
        q_t = jnp.concatenate([q_ts[A_GROUP * g + j][:, r * SUB:(r + 1) * SUB] for j in range(A_GROUP)], axis=1)
        s = jnp.dot(kn[r * SUB:r * SUB + 2 * SUB], q_t, preferred_element_type=F32)
        s = s + (bias_ref[g, variant] if r == 0 else bias_ref[g, 0])
        s_ref[c] = s
        smax_ref[c] = jnp.max(s, axis=0, keepdims=True)
    for c, (g, r) in enumerate(chains):
        m = jnp.maximum(smax_ref[c], sink_ref[g])
        p_ref[c] = jnp.exp2(s_ref[c] - m).astype(BF16)
    out_heads = [[] for _ in range(A_HEADS)]
    for c, (g, r) in enumerate(chains):
        sink = sink_ref[g]
        m = jnp.maximum(smax_ref[c], sink)
        pv = jnp.dot(v_ts[g][:, r * SUB:r * SUB + 2 * SUB], p_ref[c], preferred_element_type=F32)
        denom = pv[one_row[g]:one_row[g] + 1, :] + jnp.exp2(sink - m)
        o = pv[g * HALF:(g + 1) * HALF] * (1.0 / denom)
        for j in range(A_GROUP):
            out_heads[A_GROUP * g + j].append(o[:, j * SUB:(j + 1) * SUB])
    for c in range(A_HEADS // 2):
        pair_t = jnp.concatenate([jnp.concatenate(out_heads[2 * c], axis=1),
                                  jnp.concatenate(out_heads[2 * c + 1], axis=1)], axis=0)
        o_ref[0, :, c * LANES:(c + 1) * LANES] = pair_t.astype(o_ref.dtype).T


def _swa(qkv, gq2, e2, bias_t, sink_rows):
    b, s, _ = qkv.shape
    tq = Q_TILE
    nsub = tq // SUB
    q_w = A_HEADS * HEAD_DIM
    const = lambda bi, qi: (0, 0)
    return pl.pallas_call(
        _swa_kernel,
        grid=(b, s // tq),
        in_specs=[
            pl.BlockSpec((1, tq, q_w), lambda bi, qi: (bi, qi, 0)),
            pl.BlockSpec((1, tq, LANES), lambda bi, qi: (bi, qi, KA_BLK)),
            pl.BlockSpec((1, SUB, LANES), lambda bi, qi: (bi, jnp.maximum(qi * nsub - 1, 0), KA_BLK)),
            pl.BlockSpec((1, tq, LANES), lambda bi, qi: (bi, qi, VA_BLK)),
            pl.BlockSpec((1, SUB, LANES), lambda bi, qi: (bi, jnp.maximum(qi * nsub - 1, 0), VA_BLK)),
            pl.BlockSpec((1, LANES), const),
            pl.BlockSpec((LANES, LANES), const),
            pl.BlockSpec(bias_t.shape, lambda bi, qi: (0, 0, 0, 0)),
            pl.BlockSpec(sink_rows.shape, lambda bi, qi: (0, 0, 0)),
        ],
        out_specs=pl.BlockSpec((1, tq, q_w), lambda bi, qi: (bi, qi, 0)),
        out_shape=jax.ShapeDtypeStruct((b, s, q_w), BF16),
        scratch_shapes=[
            pltpu.VMEM((A_KV_HEADS * nsub, 2 * SUB, A_GROUP * SUB), F32),
            pltpu.VMEM((A_KV_HEADS * nsub, 1, A_GROUP * SUB), F32),
            pltpu.VMEM((A_KV_HEADS * nsub, 2 * SUB, A_GROUP * SUB), BF16),
        ],
        compiler_params=pltpu.CompilerParams(
            dimension_semantics=("arbitrary", "arbitrary"),
            vmem_limit_bytes=VMEM_ATTENTION),
        name="swa",
    )(qkv, qkv, qkv, qkv, qkv, gq2, e2, bias_t, sink_rows)


def _fox_kernel(q_ref, k_ref, v_ref, cp_ref, selq_ref, selk_ref, gq_ref, e2_ref,
                o_ref, kaug_ref, qt_ref, vt_ref, mask_ref, s_ref, smax_ref, m_ref, acc_ref):
    s_len = k_ref.shape[1]
    tq = tk = PRO_TILE
    n_tiles = s_len // tq
    e2 = e2_ref[...]
    lane = _lane_iota((1, LANES))
    low = lane < HALF
    own = (low, jnp.logical_not(low))
    one_lane = (HALF, 0)

    def prepare(i, _):
        rows = pl.ds(pl.multiple_of(i * PRO_TILE, PRO_TILE), PRO_TILE)
        kn = k_ref[0, rows, :]
        qc = q_ref[0, rows, :].astype(F32)
        qn = qc * _head_rms_scale(qc, e2) * gq_ref[...]
        vc = v_ref[0, rows, :]
        cp = cp_ref[0, rows, :]
        extra_k = jnp.dot(cp, selk_ref[0], preferred_element_type=F32)
        extra_q = jnp.dot(cp, selq_ref[0], preferred_element_type=F32)
        for hh in range(2):
            kaug_ref[hh, rows, :] = jnp.where(own[hh], kn, extra_k.astype(BF16))
            qt_ref[hh, i] = jnp.where(own[hh], qn, extra_q).astype(BF16).T
            vt_ref[hh, i] = jnp.where(own[hh], vc, jnp.where(lane == one_lane[hh], 1.0, 0.0).astype(BF16)).T
        return 0
    lax.fori_loop(0, n_tiles, prepare, 0, unroll=True)

    @pl.when(jnp.logical_and(pl.program_id(0) == 0, pl.program_id(1) == 0))
    def _build_mask():
        key = lax.broadcasted_iota(jnp.int32, (tk, tq), 0)
        qry = lax.broadcasted_iota(jnp.int32, (tk, tq), 1)
        mask_ref[...] = jnp.where(key <= qry, 0.0, NEG_BIG)

    m_ref[...] = jnp.full_like(m_ref, NEG_BIG)
    acc_ref[...] = jnp.zeros_like(acc_ref)

    def issue_scores(hh, qi, kb, masked):
        krows = pl.ds(pl.multiple_of(kb * tk, tk), tk)
        nxt = jnp.dot(kaug_ref[hh, krows, :], qt_ref[hh, qi], preferred_element_type=F32)
        if masked:
            nxt = nxt + mask_ref[...]
        s_ref[hh] = nxt
        smax_ref[hh] = jnp.max(nxt, axis=0, keepdims=True)

    half = tk // 2

    def issue_diagonal(hh, d):
        r0 = d * tk
        q_t = qt_ref[hh, d]
        top = jnp.dot(kaug_ref[hh, r0:r0 + half, :], q_t, preferred_element_type=F32) + mask_ref[:half, :]
        bot = (jnp.dot(kaug_ref[hh, r0 + half:r0 + tk, :], q_t[:, half:], preferred_element_type=F32)
               + mask_ref[half:, half:])
        s_ref[hh, :half, :] = top
        s_ref[hh, half:, half:] = bot
        s_ref[hh, half:, :half] = jnp.full((half, half), NEG_BIG, F32)
        top_max = jnp.max(top, axis=0, keepdims=True)
        smax_ref[hh] = jnp.concatenate(
            [top_max[:, :half], jnp.maximum(top_max[:, half:], jnp.max(bot, axis=0, keepdims=True))], axis=1)

    def consume(hh, qi1, kb1, s, smax):
        m_old = m_ref[hh, qi1]
        m_new = jnp.maximum(m_old, smax)
        p = jnp.exp2(s - m_new).astype(BF16)
        acc_ref[hh, qi1] = (jnp.exp2(m_old - m_new) * acc_ref[hh, qi1]
                            + jnp.dot(vt_ref[hh, kb1], p, preferred_element_type=F32))
        m_ref[hh, qi1] = m_new

    def consume_diagonal(hh, d, s_top, s_bot, smax):
        m_old = m_ref[hh, d]
        m_new = jnp.maximum(m_old, smax)
        p_top = jnp.exp2(s_top - m_new).astype(BF16)
        p_bot = jnp.exp2(s_bot - m_new[:, half:]).astype(BF16)
        v_t = vt_ref[hh, d]
        pv = jnp.dot(v_t[:, :half], p_top, preferred_element_type=F32)
        pv_late = jnp.dot(v_t[:, half:], p_bot, preferred_element_type=F32)
        acc = jnp.exp2(m_old - m_new) * acc_ref[hh, d] + pv
        acc_ref[hh, d] = jnp.concatenate([acc[:, :half], acc[:, half:] + pv_late], axis=1)
        m_ref[hh, d] = m_new

    n_off = n_tiles * (n_tiles - 1) // 2
    zero = jnp.int32(0)
    for hh in range(2):
        issue_diagonal(hh, 0)

    def off_diagonal(carry):
        qi0, kb0, qi1, kb1 = carry
        for hh in range(2):
            s_prev, smax_prev = s_ref[hh], smax_ref[hh]
            issue_scores(hh, qi0, kb0, False)
            consume(hh, qi1, kb1, s_prev, smax_prev)
        wrap = kb0 + 1 == qi0
        return jnp.where(wrap, qi0 + 1, qi0), jnp.where(wrap, 0, kb0 + 1), qi0, kb0

    def trip(_, carry):
        for _ in range(FOX_STEPS_PER_TRIP):
            carry = off_diagonal(carry)
        return carry
    carry = (jnp.int32(1), zero, zero, zero)
    carry = lax.fori_loop(0, n_off // FOX_STEPS_PER_TRIP, trip, carry)
    for _ in range(n_off % FOX_STEPS_PER_TRIP):
        carry = off_diagonal(carry)
    for d in range(1, n_tiles):
        for hh in range(2):
            if d == 1:
                s_prev, smax_prev = s_ref[hh], smax_ref[hh]
                issue_diagonal(hh, d)
                consume(hh, carry[2], carry[3], s_prev, smax_prev)
            else:
                s_top, s_bot, smax_prev = s_ref[hh, :half, :], s_ref[hh, half:, half:], smax_ref[hh]
                issue_diagonal(hh, d)
                consume_diagonal(hh, d - 1, s_top, s_bot, smax_prev)
    for hh in range(2):
        consume_diagonal(hh, n_tiles - 1, s_ref[hh, :half, :], s_ref[hh, half:, half:], smax_ref[hh])

    feat = lax.broadcasted_iota(jnp.int32, (LANES, 1), 0)

    def finish(i, _):
        outs = []
        for hh in range(2):
            acc = acc_ref[hh, i]
            outs.append(acc * (1.0 / acc[one_lane[hh]:one_lane[hh] + 1, :]))
        rows = pl.ds(pl.multiple_of(i * tq, tq), tq)
        o_ref[0, rows, :] = jnp.where(feat < HALF, outs[0], outs[1]).astype(o_ref.dtype).T
        return 0
    lax.fori_loop(0, n_tiles, finish, 0, unroll=True)


def _fox(qkv, cp, selq, selk, gq2, e2):
    b, s, _ = qkv.shape
    assert s % PRO_TILE == 0
    npair = B_HEADS // 2
    nblk = s // PRO_TILE
    const = lambda bi, p: (0, 0)
    return pl.pallas_call(
        _fox_kernel,
        grid=(b, npair),
        in_specs=[
            pl.BlockSpec((1, s, LANES), lambda bi, p: (bi, 0, QB_BLK + p)),
            pl.BlockSpec((1, s, LANES), lambda bi, p: (bi, 0, KB_BLK + p)),
            pl.BlockSpec((1, s, LANES), lambda bi, p: (bi, 0, VB_BLK + p)),
            pl.BlockSpec((1, s, LANES), lambda bi, p: (bi, 0, 0)),
            pl.BlockSpec((1, LANES, LANES), lambda bi, p: (p, 0, 0)),
            pl.BlockSpec((1, LANES, LANES), lambda bi, p: (p, 0, 0)),
            pl.BlockSpec((1, LANES), const),
            pl.BlockSpec((LANES, LANES), const),
        ],
        out_specs=pl.BlockSpec((1, s, LANES), lambda bi, p: (bi, 0, p)),
        out_shape=jax.ShapeDtypeStruct((b, s, B_HEADS * HEAD_DIM), BF16),
        scratch_shapes=[
            pltpu.VMEM((2, s, LANES), BF16),
            pltpu.VMEM((2, nblk, LANES, PRO_TILE), BF16),
            pltpu.VMEM((2, nblk, LANES, PRO_TILE), BF16),
            pltpu.VMEM((PRO_TILE, PRO_TILE), F32),
            pltpu.VMEM((2, PRO_TILE, PRO_TILE), F32),
            pltpu.VMEM((2, 1, PRO_TILE), F32),
            pltpu.VMEM((2, nblk, 1, PRO_TILE), F32),
            pltpu.VMEM((2, nblk, LANES, PRO_TILE), F32),
        ],
        compiler_params=pltpu.CompilerParams(
            dimension_semantics=("arbitrary", "arbitrary"),
            vmem_limit_bytes=VMEM_ATTENTION),
        name="fox",
    )(qkv, qkv, qkv, cp, selq, selk, gq2, e2)


def _out_mlp_kernel(x_ref, oa_ref, ob_ref, woa_ref, wob_ref, g_ref, wup_ref, wdn_ref, y_ref):
    h = (x_ref[...]
         + jnp.dot(oa_ref[...], woa_ref[...], preferred_element_type=F32)
         + jnp.dot(ob_ref[...], wob_ref[...], preferred_element_type=F32))
    ms = jnp.sum(h * h, axis=-1, keepdims=True) * (1.0 / D_MODEL)
    hn = (h * lax.rsqrt(ms + EPS) * g_ref[...]).astype(BF16)
    u = jnp.maximum(jnp.dot(hn, wup_ref[...], preferred_element_type=F32), 0.0)
    y_ref[...] = h + jnp.dot((u * u).astype(BF16), wdn_ref[...], preferred_element_type=F32)


def _out_mlp(x2, oa2, ob2, woa, wob, g2, wup, wdn):
    n, d = x2.shape
    tm = TOK_TILE
    const = lambda i: (0, 0)
    resident = dict(pipeline_mode=pl.Buffered(1))
    return pl.pallas_call(
        _out_mlp_kernel,
        grid=(n // tm,),
        in_specs=[
            pl.BlockSpec((tm, d), lambda i: (i, 0)),
            pl.BlockSpec((tm, oa2.shape[1]), lambda i: (i, 0)),
            pl.BlockSpec((tm, ob2.shape[1]), lambda i: (i, 0)),
            pl.BlockSpec(woa.shape, const, **resident),
            pl.BlockSpec(wob.shape, const, **resident),
            pl.BlockSpec((1, d), const),
            pl.BlockSpec(wup.shape, const, **resident),
            pl.BlockSpec(wdn.shape, const, **resident),
        ],
        out_specs=pl.BlockSpec((tm, d), lambda i: (i, 0)),
        out_shape=jax.ShapeDtypeStruct((n, d), F32),
        compiler_params=pltpu.CompilerParams(
            dimension_semantics=("arbitrary",),
            vmem_limit_bytes=VMEM_OUT_MLP),
        name="out_mlp",
    )(x2, oa2, ob2, woa, wob, g2, wup, wdn)


def _block_diag_ones():
    i = np.arange(LANES)
    return jnp.asarray((i[:, None] // HALF == i[None, :] // HALF).astype(np.float32), dtype=BF16)


def _fox_select_tables():
    selq = np.zeros((B_HEADS // 2, LANES, LANES), np.float32)
    selk = np.zeros_like(selq)
    for p in range(B_HEADS // 2):
        for hh in range(2):
            h = 2 * p + hh
            base = HALF if hh == 0 else 0
            for piece in range(3):
                selq[p, piece * PIECE_STRIDE + h, base + piece] = 1.0
                selq[p, ONE_LANE, base + 3 + piece] = 1.0
                selk[p, ONE_LANE, base + piece] = 1.0
                selk[p, piece * PIECE_STRIDE + h, base + 3 + piece] = -1.0
    return jnp.asarray(selq, dtype=BF16), jnp.asarray(selk, dtype=BF16)


def _swa_bias_tables():
    slopes = np.exp2(-(8.0 / A_HEADS) * (np.arange(A_HEADS, dtype=np.float32) + 1.0)).astype(np.float32)
    qpos = SUB + np.arange(SUB)
    kpos = np.arange(2 * SUB)
    dist = kpos[:, None] * 0 + qpos[None, :] - kpos[:, None]
    band = (dist >= 0) & (dist < WINDOW)
    out = np.zeros((A_KV_HEADS, 2, 2 * SUB, A_GROUP * SUB), np.float32)
    for g in range(A_KV_HEADS):
        for j in range(A_GROUP):
            alibi = (-slopes[g * A_GROUP + j] * dist.astype(np.float32)).astype(np.float32) * np.float32(LOG2E)
            cols = slice(j * SUB, (j + 1) * SUB)
            out[g, 0, :, cols] = np.where(band, alibi, NEG_BIG)
            out[g, 1, :, cols] = np.where(band & (kpos >= SUB)[:, None], alibi, NEG_BIG)
    return jnp.asarray(out)


def kernel(x, attn_norm_g, w_in, b_forget, q_norm_a, k_norm_a, sink_logits, q_norm_b, k_norm_b, w_out,
           mlp_norm_g, w_up, w_down):
    b, s, d = x.shape
    assert d == D_MODEL and s % Q_TILE == 0 and s % PROJ_TILE == 0 and (b * s) % TOK_TILE == 0
    scale = 1.0 / math.sqrt(HEAD_DIM)

    w_pad = jnp.pad(w_in, ((0, 0), (0, PROJ_W - w_in.shape[1]))).astype(BF16)
    bf_pad = jnp.pad(b_forget.astype(F32), (0, LANES - B_HEADS)).reshape(1, LANES)
    e2 = _block_diag_ones()
    selq, selk = _fox_select_tables()
    bias = _swa_bias_tables()
    sink_rows = jnp.repeat(sink_logits.astype(F32).reshape(A_KV_HEADS, 1, A_GROUP) * LOG2E, SUB, axis=2)
    root = math.sqrt(HEAD_DIM)
    gqa2 = (jnp.tile(q_norm_a.astype(F32), 2) * (root * scale * LOG2E)).reshape(1, LANES)
    gka2 = (jnp.tile(k_norm_a.astype(F32), 2) * root).reshape(1, LANES)
    gqb2 = (jnp.tile(q_norm_b.astype(F32), 2) * (root * scale * LOG2E)).reshape(1, LANES)
    gkb2 = (jnp.tile(k_norm_b.astype(F32), 2) * root).reshape(1, LANES)

    qkv, cp = _in_proj(x, attn_norm_g.astype(F32).reshape(1, d), w_pad, bf_pad, gka2, gkb2)
    out_a = _swa(qkv, gqa2, e2, bias, sink_rows)
    out_b = _fox(qkv, cp, selq, selk, gqb2, e2)

    wo = w_out.astype(BF16)
    y = _out_mlp(x.reshape(b * s, d), out_a.reshape(b * s, -1), out_b.reshape(b * s, -1),
                 wo[:A_HEADS * HEAD_DIM], wo[A_HEADS * HEAD_DIM:],
                 mlp_norm_g.astype(F32).reshape(1, d), w_up.astype(BF16), w_down.astype(BF16))
    return y.reshape(b, s, d)
```
